```python
import jax
import jax.numpy as jnp
from jax import lax
import numpy as np

D_MODEL = 4096
BATCH = 4
SEQ = 2048
DEPTH = 4
DEC_BATCH = 8
DEC_SEQ = 64
PAST_LEN = 1024

CHUNK = 64
N_META = 16
N_A_LAYERS = DEPTH // 2
N_B_LAYERS = DEPTH - N_A_LAYERS
HEAD_A = 64
N_HEADS_A = D_MODEL // HEAD_A
LORA_DECAY = 128
LORA_AAA = 128
LORA_MV = 96
LORA_GATE = 480
GN_EPS = 64e-5
HEAD_B = 128
N_HEADS_B = D_MODEL // HEAD_B
SB_BLOCK = 128
D_FF = 11008
CONV_W = 3
RMS_EPS = 1e-6

kernel_name = 'yoco_rwkv7_stickbreaking_convffn_step'


def rmsnorm(x, g):
    xf = x.astype(jnp.float32)
    y = xf * lax.rsqrt(jnp.mean(xf * xf, axis=-1, keepdims=True) + RMS_EPS)
    return (y * g.astype(jnp.float32)).astype(x.dtype)


def wkv7_scan(S0, r, decay, k, v, kk, a):
    def step(S, inp):
        r_t, d_t, k_t, v_t, kk_t, a_t = inp
        sa = jnp.einsum('bhvk,bhk->bhv', S, -kk_t)
        S = (S * d_t[:, :, None, :]
             + sa[..., None] * (kk_t * a_t)[:, :, None, :]
             + v_t[..., None] * k_t[:, :, None, :])
        o = jnp.einsum('bhvk,bhk->bhv', S, r_t)
        return S, o
    xs = tuple(jnp.moveaxis(t, 1, 0) for t in (r, decay, k, v, kk, a))
    S, o = lax.scan(step, S0, xs)
    return jnp.moveaxis(o, 0, 1), S


def rwkv7_time_mix(x, prev_row, S0, v_first, mix, w_rkv, w_out, w0, w1, w2, a0, a1, a2, vres,
                   g1, g2, k_k, k_a, r_k, ln_w, ln_b):
    B, T, D = x.shape
    f32 = jnp.float32
    x_prev = jnp.concatenate([prev_row[:, None, :].astype(x.dtype), x[:, :-1]], axis=1)
    xx = x_prev - x
    xr, xw, xk, xv, xa, xg = (x + xx * mix[i] for i in range(6))
    r = xr @ w_rkv[0]
    k = xk @ w_rkv[1]
    v = xv @ w_rkv[2]
    w_log = -jax.nn.softplus(-(w0 + jnp.tanh(xw @ w1) @ w2).astype(f32)) - 0.5
    a = jax.nn.sigmoid((a0 + (xa @ a1) @ a2).astype(f32))
    g = jax.nn.sigmoid(xg @ g1) @ g2
    if vres is None:
        v_first = v
    else:
        v0, v1, v2 = vres
        v = v + (v_first - v) * jax.nn.sigmoid(v0 + (xv @ v1) @ v2)
    heads = lambda t: t.astype(f32).reshape(B, T, N_HEADS_A, HEAD_A)
    hp = lambda t: t.astype(f32).reshape(N_HEADS_A, HEAD_A)
    r_h, k_h, v_h, a_h, w_h = heads(r), heads(k), heads(v), heads(a), heads(w_log)
    kk = k_h * hp(k_k)
    kk = kk / jnp.maximum(jnp.linalg.norm(kk, axis=-1, keepdims=True), 1e-12)
    k_h = k_h * (1.0 + (a_h - 1.0) * hp(k_a))
    decay = jnp.exp(-jnp.exp(w_h))
    o, S = wkv7_scan(S0.astype(f32), r_h, decay, k_h, v_h, kk, a_h)
    mu = jnp.mean(o, axis=-1, keepdims=True)
    var = jnp.mean(jnp.square(o - mu), axis=-1, keepdims=True)
    o = (o - mu) * lax.rsqrt(var + GN_EPS) * hp(ln_w) + hp(ln_b)
    o = o + jnp.sum(r_h * k_h * r_k.astype(f32), axis=-1, keepdims=True) * v_h
    out = (o.reshape(B, T, D).astype(x.dtype) * g) @ w_out
    return out, S.astype(S0.dtype), v_first, x[:, -1]


def conv_ffn(x, conv_state, w_up, conv_w, conv_b, w_down):
    T = x.shape[1]
    gate, up = jnp.split(x @ w_up, 2, axis=-1)
    gp = jnp.concatenate([conv_state.astype(gate.dtype), gate], axis=1)
    conv = sum((gp[:, i:i + T] * conv_w[i] for i in range(CONV_W)), conv_b)
    y = (jax.nn.silu(conv) * up) @ w_down
    return y, gp[:, T:]


def stick_breaking_block(q, k, v, q_start):
    f32 = jnp.float32
    z = jnp.einsum('bqhd,bkhd->bhqk', q.astype(f32), k.astype(f32)) * (HEAD_B ** -0.5)
    q_pos = q_start + jnp.arange(q.shape[1])
    k_pos = jnp.arange(k.shape[1])
    read = k_pos[None, :] < q_pos[:, None]
    log_keep = jnp.where(read, jax.nn.log_sigmoid(-z), 0.0)
    suffix = jnp.flip(jnp.cumsum(jnp.flip(log_keep, -1), axis=-1), -1)
    after = jnp.concatenate([suffix[..., 1:], jnp.zeros_like(suffix[..., :1])], axis=-1)
    wts = jnp.where(read, jnp.exp(jax.nn.log_sigmoid(z) + after), 0.0)
    return jnp.einsum('bhqk,bkhd->bqhd', wts, v.astype(f32)).astype(q.dtype)


def stick_breaking_attention(q, k, v, q_base):
    T = q.shape[1]
    outs = []
    for start in range(0, T, SB_BLOCK):
        end = min(T, start + SB_BLOCK)
        outs.append(stick_breaking_block(q[:, start:end], k[:, :q_base + end], v[:, :q_base + end],
                                         q_base + start))
    return jnp.concatenate(outs, axis=1)


def run_group(h, shift0, wkv0, conv0, past_k, past_v, p):
    B, T, _ = h.shape
    q_base = past_k.shape[1]
    new_wkv, new_shift, new_conv = [], [], []
    v_first = None
    k_all = v_all = k_new = v_new = None
    for layer in range(DEPTH):
        g = p['norm_gains'][layer]
        xin = rmsnorm(h, g[0])
        if layer < N_A_LAYERS:
            i = layer
            vres = None if i == 0 else (p['a_v0'][i - 1], p['a_v1'][i - 1], p['a_v2'][i - 1])
            mixed, S, v_first, last_row = rwkv7_time_mix(
                xin, shift0[i], wkv0[i], v_first, p['a_mix'][i], p['a_w_rkv'][i], p['a_w_out'][i],
                p['a_w0'][i], p['a_w1'][i], p['a_w2'][i], p['a_a0'][i], p['a_a1'][i], p['a_a2'][i], vres,
                p['a_g1'][i], p['a_g2'][i], p['a_k_k'][i], p['a_k_a'][i], p['a_r_k'][i],
                p['a_ln_w'][i], p['a_ln_b'][i])
            new_wkv.append(S)
            new_shift.append(last_row)
        else:
            if layer == N_A_LAYERS:
                kv = rmsnorm(h, p['kv_norm']) @ p['w_kv']
                k_flat, v_flat = jnp.split(kv, 2, axis=-1)
                k_new = k_flat.reshape(B, T, N_HEADS_B, HEAD_B)
                v_new = v_flat.reshape(B, T, N_HEADS_B, HEAD_B)
                k_all = jnp.concatenate([past_k.astype(k_new.dtype), k_new], axis=1)
                v_all = jnp.concatenate([past_v.astype(v_new.dtype), v_new], axis=1)
            j = layer - N_A_LAYERS
            q = (xin @ p['b_w_q'][j]).reshape(B, T, N_HEADS_B, HEAD_B)
            attn = stick_breaking_attention(q, k_all, v_all, q_base)
            mixed = attn.reshape(B, T, D_MODEL) @ p['b_w_out'][j]
        h = h + rmsnorm(mixed, g[1])
        f, c = conv_ffn(rmsnorm(h, g[2]), conv0[layer], p['ffn_w_up'][layer], p['ffn_conv_w'][layer],
                        p['ffn_conv_b'][layer], p['ffn_w_down'][layer])
        new_conv.append(c)
        h = h + rmsnorm(f, g[3])
    return h, k_new, v_new, jnp.stack(new_wkv), jnp.stack(new_shift), jnp.stack(new_conv)


def setup_inputs(seed: int = 0) -> dict:
    key = jax.random.key(seed)
    D, F = D_MODEL, D_FF
    nA, nB = N_A_LAYERS, N_B_LAYERS

    def nrm(i, shape, scale):
        return jax.random.normal(jax.random.fold_in(key, i), shape, jnp.float32) * scale

    def unif(i, shape, lo, hi):
        return jax.random.uniform(jax.random.fold_in(key, i), shape, jnp.float32, lo, hi)

    return {
        'x_prompt': nrm(0, (BATCH, SEQ, D), 1.0),
        'x_sample': nrm(1, (DEC_BATCH, DEC_SEQ, D), 1.0),
        'cache_k': nrm(2, (DEC_BATCH, PAST_LEN, N_HEADS_B, HEAD_B), 1.0),
        'cache_v': nrm(3, (DEC_BATCH, PAST_LEN, N_HEADS_B, HEAD_B), 1.0),
        'state_wkv': nrm(4, (nA, DEC_BATCH, N_HEADS_A, HEAD_A, HEAD_A), 0.5),
        'state_shift': nrm(5, (nA, DEC_BATCH, D), 1.0),
        'state_ffn_conv': nrm(6, (DEPTH, DEC_BATCH, CONV_W - 1, F), 1.0),
        'meta_tokens': nrm(7, (N_META, D), 1.0),
        'norm_gains': 1.0 + nrm(8, (DEPTH, 4, D), 0.02),
        'a_mix': unif(9, (nA, 6, D), 0.0, 1.0),
        'a_w_rkv': nrm(10, (nA, 3, D, D), D ** -0.5),
        'a_w_out': nrm(11, (nA, D, D), D ** -0.5),
        'a_w0': unif(12, (nA, D), -3.0, 1.0),
        'a_w1': nrm(13, (nA, D, LORA_DECAY), D ** -0.5),
        'a_w2': nrm(14, (nA, LORA_DECAY, D), 0.1 * LORA_DECAY ** -0.5),
        'a_a0': nrm(15, (nA, D), 0.1),
        'a_a1': nrm(16, (nA, D, LORA_AAA), D ** -0.5),
        'a_a2': nrm(17, (nA, LORA_AAA, D), 0.1 * LORA_AAA ** -0.5),
        'a_v0': nrm(18, (nA - 1, D), 0.1),
        'a_v1': nrm(19, (nA - 1, D, LORA_MV), D ** -0.5),
        'a_v2': nrm(20, (nA - 1, LORA_MV, D), 0.1 * LORA_MV ** -0.5),
        'a_g1': nrm(21, (nA, D, LORA_GATE), D ** -0.5),
        'a_g2': nrm(22, (nA, LORA_GATE, D), LORA_GATE ** -0.5),
        'a_k_k': 0.85 + nrm(23, (nA, D), 0.05),
        'a_k_a': 1.0 + nrm(24, (nA, D), 0.05),
        'a_r_k': nrm(25, (nA, N_HEADS_A, HEAD_A), 0.1),
        'a_ln_w': 1.0 + nrm(26, (nA, D), 0.02),
        'a_ln_b': nrm(27, (nA, D), 0.01),
        'kv_norm': 1.0 + nrm(28, (D,), 0.02),
        'w_kv': nrm(29, (D, 2 * D), D ** -0.5),
        'b_w_q': nrm(30, (nB, D, D), D ** -0.5),
        'b_w_out': nrm(31, (nB, D, D), D ** -0.5),
        'ffn_w_up': nrm(32, (DEPTH, D, 2 * F), D ** -0.5),
        'ffn_conv_w': nrm(33, (DEPTH, CONV_W, F), CONV_W ** -0.5),
        'ffn_conv_b': nrm(34, (DEPTH, F), 0.01),
        'ffn_w_down': nrm(35, (DEPTH, F, D), F ** -0.5),
    }


def reference(x_prompt, x_sample, cache_k, cache_v, state_wkv, state_shift, state_ffn_conv,
              meta_tokens, norm_gains, a_mix, a_w_rkv, a_w_out, a_w0, a_w1, a_w2, a_a0, a_a1, a_a2,
              a_v0, a_v1, a_v2, a_g1, a_g2, a_k_k, a_k_a, a_r_k, a_ln_w, a_ln_b, kv_norm, w_kv,
              b_w_q, b_w_out, ffn_w_up, ffn_conv_w, ffn_conv_b, ffn_w_down):
    p = {
        'norm_gains': norm_gains, 'a_mix': a_mix, 'a_w_rkv': a_w_rkv, 'a_w_out': a_w_out,
        'a_w0': a_w0, 'a_w1': a_w1, 'a_w2': a_w2, 'a_a0': a_a0, 'a_a1': a_a1, 'a_a2': a_a2,
        'a_v0': a_v0, 'a_v1': a_v1, 'a_v2': a_v2, 'a_g1': a_g1, 'a_g2': a_g2,
        'a_k_k': a_k_k, 'a_k_a': a_k_a, 'a_r_k': a_r_k, 'a_ln_w': a_ln_w, 'a_ln_b': a_ln_b,
        'kv_norm': kv_norm, 'w_kv': w_kv, 'b_w_q': b_w_q, 'b_w_out': b_w_out,
        'ffn_w_up': ffn_w_up, 'ffn_conv_w': ffn_conv_w, 'ffn_conv_b': ffn_conv_b,
        'ffn_w_down': ffn_w_down,
    }
    dt = x_prompt.dtype
    B = x_prompt.shape[0]
    meta = jnp.broadcast_to(meta_tokens[None].astype(dt), (B, N_META, D_MODEL))
    h0 = jnp.concatenate([meta, x_prompt], axis=1)
    zero_shift = jnp.zeros((N_A_LAYERS, B, D_MODEL), dt)
    zero_wkv = jnp.zeros((N_A_LAYERS, B, N_HEADS_A, HEAD_A, HEAD_A), dt)
    zero_conv = jnp.zeros((DEPTH, B, CONV_W - 1, D_FF), dt)
    no_past = jnp.zeros((B, 0, N_HEADS_B, HEAD_B), dt)
    h_p, k_p, v_p, wkv_p, shift_p, conv_p = run_group(h0, zero_shift, zero_wkv, zero_conv,
                                                      no_past, no_past, p)
    y_prompt = h_p[:, N_META:]
    y_sample, k_s, v_s, wkv_s, shift_s, conv_s = run_group(x_sample, state_shift, state_wkv,
                                                           state_ffn_conv, cache_k, cache_v, p)
    return (y_prompt, y_sample, k_p, v_p, k_s, v_s, wkv_p, wkv_s, shift_p, shift_s, conv_p, conv_s)
```

```python
import functools

import jax
import jax.numpy as jnp
from jax import lax
from jax.experimental import pallas as pl
from jax.experimental.pallas import tpu as pltpu

F32 = jnp.float32
BF16 = jnp.bfloat16

N_META = 16
HEAD_A = 64
HEAD_B = 128
GN_EPS = 64e-5
RMS_EPS = 1e-6
CONV_W = 3

V7X_LANES = 128
V7X_VMEM_BYTES = 64 * 1024 * 1024
VMEM_LIMIT = V7X_VMEM_BYTES - 8 * 1024 * 1024

WKV_GROUP = 2
WKV_LANES = WKV_GROUP * HEAD_A
WKV_CHUNK = 64
ATT_BLOCK = 128


def _pick(n, prefs):
    for p in prefs:
        if p <= n and n % p == 0:
            return p
    return n


def _cparams(sem):
    return pltpu.CompilerParams(dimension_semantics=sem, vmem_limit_bytes=VMEM_LIMIT)


def _bdot(a, b):
    return jnp.dot(a.astype(BF16), b.astype(BF16), preferred_element_type=F32)


def _bdot_nt(a, b):
    return lax.dot_general(a.astype(BF16), b.astype(BF16), (((1,), (1,)), ((), ())),
                           preferred_element_type=F32)


def _bdot_tn(a, b):
    return lax.dot_general(a.astype(BF16), b.astype(BF16), (((0,), (0,)), ((), ())),
                           preferred_element_type=F32)


def _split_dot(x, m_bf16, parts):
    acc = None
    rem = x
    for _ in range(parts):
        hi = rem.astype(BF16)
        t = jnp.dot(hi, m_bf16, preferred_element_type=F32)
        acc = t if acc is None else acc + t
        rem = rem - hi.astype(F32)
    return acc


def _rms(x, g):
    return x * lax.rsqrt(jnp.mean(x * x, axis=-1, keepdims=True) + RMS_EPS) * g


def _epi_none(acc):
    return acc


def _epi_tanh(acc):
    return jnp.tanh(acc)


def _epi_sigmoid(acc):
    return jax.nn.sigmoid(acc)


def _epi_bias_sigmoid(acc, bias):
    return jax.nn.sigmoid(bias + acc)


def _epi_log_decay(acc, w0):
    return -jnp.exp(-jax.nn.softplus(-(w0 + acc)) - 0.5)


def _epi_value_mix(acc, v0, v, v_first):
    return v + (v_first - v) * jax.nn.sigmoid(v0 + acc)


def _mm_kernel(*refs, epilogue, n_row, n_tile):
    x_ref, w_ref = refs[0], refs[1]
    rows = [r[...] for r in refs[2:2 + n_row]]
    tiles = [r[...] for r in refs[2 + n_row:2 + n_row + n_tile]]
    o_ref = refs[2 + n_row + n_tile]
    acc = jnp.dot(x_ref[...], w_ref[...], preferred_element_type=F32)
    o_ref[...] = epilogue(acc, *rows, *tiles).astype(o_ref.dtype)


def _matmul(x, w, *, out_dtype, epilogue=_epi_none, rows=(), tiles=(), name):
    M, K = x.shape
    N = w.shape[1]
    tm = _pick(M, (2064, 1376, 688, 512, 256, 128, 64, 16))
    if K > 8192:
        tm = _pick(M, (688, 512, 256, 128, 64, 16))
    tn = _pick(N, (512, 256, 128))
    x_mode = pl.Buffered(1) if tm * K * 2 > 8 * 1024 * 1024 else None
    in_specs = [pl.BlockSpec((tm, K), lambda i, j: (i, 0), pipeline_mode=x_mode),
                pl.BlockSpec((K, tn), lambda i, j: (0, j))]
    in_specs += [pl.BlockSpec((1, tn), lambda i, j: (0, j)) for _ in rows]
    in_specs += [pl.BlockSpec((tm, tn), lambda i, j: (i, j)) for _ in tiles]
    return pl.pallas_call(
        functools.partial(_mm_kernel, epilogue=epilogue, n_row=len(rows), n_tile=len(tiles)),
        grid=(M // tm, N // tn),
        in_specs=in_specs,
        out_specs=pl.BlockSpec((tm, tn), lambda i, j: (i, j)),
        out_shape=jax.ShapeDtypeStruct((M, N), out_dtype),
        compiler_params=_cparams(("parallel", "arbitrary")),
        name=name,
    )(x, w, *rows, *tiles)


def _resnorm_kernel(*refs, n_next):
    h_ref, y_ref, gp_ref = refs[0], refs[1], refs[2]
    gn_refs = refs[3:3 + n_next]
    hn_ref = refs[3 + n_next]
    xo_refs = refs[4 + n_next:]
    hn = h_ref[...] + _rms(y_ref[...], gp_ref[...])
    hn_ref[...] = hn
    for gn_ref, xo_ref in zip(gn_refs, xo_refs):
        xo_ref[...] = _rms(hn, gn_ref[...]).astype(xo_ref.dtype)


def _resnorm(h, y, g_post, g_next):
    M, D = h.shape
    tr = _pick(M, (192, 128, 64, 16))
    n = len(g_next)
    row = pl.BlockSpec((tr, D), lambda i: (i, 0))
    vec = pl.BlockSpec((1, D), lambda i: (0, 0))
    outs = pl.pallas_call(
        functools.partial(_resnorm_kernel, n_next=n),
        grid=(M // tr,),
        in_specs=[row, row, vec] + [vec] * n,
        out_specs=[row] + [row] * n,
        out_shape=[jax.ShapeDtypeStruct((M, D), F32)] + [jax.ShapeDtypeStruct((M, D), BF16)] * n,
        compiler_params=_cparams(("parallel",)),
        name="resnorm",
    )(h, y, g_post.reshape(1, D), *[g.reshape(1, D) for g in g_next])
    return outs[0], list(outs[1:])


def _norm_kernel(h_ref, g_ref, o_ref):
    o_ref[...] = _rms(h_ref[...], g_ref[...]).astype(o_ref.dtype)


def _norm(h, g):
    M, D = h.shape
    tr = _pick(M, (192, 128, 64, 16))
    return pl.pallas_call(
        _norm_kernel,
        grid=(M // tr,),
        in_specs=[pl.BlockSpec((tr, D), lambda i: (i, 0)), pl.BlockSpec((1, D), lambda i: (0, 0))],
        out_specs=pl.BlockSpec((tr, D), lambda i: (i, 0)),
        out_shape=jax.ShapeDtypeStruct((M, D), BF16),
        compiler_params=_cparams(("parallel",)),
        name="norm",
    )(h, g.reshape(1, D))


def _normmix_kernel(h_ref, halo_ref, g_ref, shift_ref, mix_ref, *out_refs, tt):
    t = pl.program_id(1)
    g = g_ref[...]
    xn = _rms(h_ref[0], g)
    halo = _rms(halo_ref[0], g)
    prev_row = jnp.where(t == 0, shift_ref[0], halo[7:8, :])
    row = lax.broadcasted_iota(jnp.int32, (tt, 1), 0)
    x_prev = jnp.where(row == 0, prev_row, pltpu.roll(xn, 1, axis=0))
    xx = x_prev - xn
    for i in range(6):
        out_refs[i][0] = (xn + xx * mix_ref[i:i + 1, :]).astype(BF16)
    out_refs[6][0] = xn[tt - 1:tt, :]


def _normmix(h, g, shift0, mix):
    B, T, D = h.shape
    tt = _pick(T, (64, 48, 32, 16))
    nb = tt // 8
    tok = pl.BlockSpec((1, tt, D), lambda b, t: (b, t, 0))
    outs = pl.pallas_call(
        functools.partial(_normmix_kernel, tt=tt),
        grid=(B, T // tt),
        in_specs=[tok,
                  pl.BlockSpec((1, 8, D), lambda b, t: (b, jnp.maximum(t * nb - 1, 0), 0)),
                  pl.BlockSpec((1, D), lambda b, t: (0, 0)),
                  pl.BlockSpec((1, 1, D), lambda b, t: (b, 0, 0)),
                  pl.BlockSpec((6, D), lambda b, t: (0, 0))],
        out_specs=[tok] * 6 + [pl.BlockSpec((1, 1, D), lambda b, t: (b, 0, 0))],
        out_shape=[jax.ShapeDtypeStruct((B, T, D), BF16)] * 6 + [jax.ShapeDtypeStruct((B, 1, D), F32)],
        compiler_params=_cparams(("parallel", "arbitrary")),
        name="normmix",
    )(h, h, g.reshape(1, D), shift0.reshape(B, 1, D), mix)
    return list(outs[:6]), outs[6].reshape(B, D)


def _wkv_chunk(c0, C, refs, consts, st_scr):
    (r_ref, k_ref, v_ref, ld_ref, a_ref, g_ref, y_ref) = refs
    (kkw, kaw, rkw, lnw, lnb, bones, lane_head) = consts
    G, LW = WKV_GROUP, WKV_LANES
    R = G * C
    sl = pl.ds(c0, C)
    r = r_ref[0, sl, :]
    k = k_ref[0, sl, :]
    v = v_ref[0, sl, :]
    ld = ld_ref[0, sl, :]
    a = a_ref[0, sl, :]

    kk = k * kkw
    kk = kk / jnp.maximum(jnp.sqrt(_split_dot(kk * kk, bones, 2)), 1e-12)
    km = k * (1.0 + (a - 1.0) * kaw)
    bonus = _split_dot(r * km * rkw, bones, 2) * v
    b = kk * a

    ti = lax.broadcasted_iota(jnp.int32, (C, C), 0)
    si = lax.broadcasted_iota(jnp.int32, (C, C), 1)
    cum = _split_dot_lhs(jnp.where(ti >= si, 1.0, 0.0).astype(BF16), ld)
    cum_last = cum[C - 1:C, :]
    e_neg = jnp.exp(-cum)
    e_rem = jnp.exp(cum_last - cum)
    a_t = jnp.exp(cum - ld) * (-kk)
    r_t = jnp.exp(cum) * r
    b_t = e_neg * b
    k_t = e_neg * km
    b_h = e_rem * b
    k_h = e_rem * km

    def stack(x):
        return jnp.concatenate([jnp.where(lane_head == h, x, 0.0) for h in range(G)], axis=0)

    a_s, r_s, b_s, k_s, v_s, bh_s, kh_s = (stack(x) for x in (a_t, r_t, b_t, k_t, v, b_h, k_h))

    row = lax.broadcasted_iota(jnp.int32, (R, R), 0)
    col = lax.broadcasted_iota(jnp.int32, (R, R), 1)
    same = (row // C) == (col // C)
    strict = same & (row > col)
    incl = same & (row >= col)
    n_ab = jnp.where(strict, _bdot_nt(a_s, b_s), 0.0)
    a_ak = jnp.where(strict, _bdot_nt(a_s, k_s), 0.0)
    a_rb = jnp.where(incl, _bdot_nt(r_s, b_s), 0.0)
    a_rk = jnp.where(incl, _bdot_nt(r_s, k_s), 0.0)

    eye = jnp.where(row == col, 1.0, 0.0)
    tinv = eye + n_ab
    npow = n_ab
    span = 2
    while span < C:
        npow = _bdot(npow, npow)
        tinv = tinv + _bdot(tinv, npow)
        span *= 2

    w_s = _bdot(tinv, a_s)
    u0 = _bdot(tinv, _bdot(a_ak, v_s))
    q_s = r_s + _bdot(a_rb, w_s)
    o0 = _bdot(a_rb, u0) + _bdot(a_rk, v_s)
    li = lax.broadcasted_iota(jnp.int32, (LW, LW), 0)
    lj = lax.broadcasted_iota(jnp.int32, (LW, LW), 1)
    m1 = jnp.where(li == lj, jnp.exp(cum_last), 0.0) + _bdot_tn(bh_s, w_s)
    m2 = _bdot_tn(bh_s, u0) + _bdot_tn(kh_s, v_s)

    s_old = st_scr[...]
    o_st = _bdot(q_s, s_old) + o0
    st_scr[...] = _bdot(m1, s_old) + m2

    o = o_st[0:C, :]
    for h in range(1, G):
        o = o + o_st[h * C:(h + 1) * C, :]

    mu = _split_dot(o, bones, 2) * (1.0 / HEAD_A)
    d = o - mu
    var = _split_dot(d * d, bones, 2) * (1.0 / HEAD_A)
    o = d * lax.rsqrt(var + GN_EPS) * lnw + lnb + bonus
    y_ref[0, sl, :] = (o * g_ref[0, sl, :]).astype(y_ref.dtype)


def _split_dot_lhs(m_bf16, x):
    acc = None
    rem = x
    for _ in range(3):
        hi = rem.astype(BF16)
        t = jnp.dot(m_bf16, hi, preferred_element_type=F32)
        acc = t if acc is None else acc + t
        rem = rem - hi.astype(F32)
    return acc


def _wkv_kernel(r_ref, k_ref, v_ref, ld_ref, a_ref, g_ref, kkw_ref, kaw_ref, rkw_ref, lnw_ref, lnb_ref,
                s0_ref, y_ref, sout_ref, st_scr, *, T):
    G, LW, N = WKV_GROUP, WKV_LANES, HEAD_A
    li = lax.broadcasted_iota(jnp.int32, (LW, LW), 0)
    lj = lax.broadcasted_iota(jnp.int32, (LW, LW), 1)
    bones = jnp.where((li // N) == (lj // N), 1.0, 0.0).astype(BF16)
    lane_head = lax.broadcasted_iota(jnp.int32, (1, LW), 1) // N
    consts = (kkw_ref[...], kaw_ref[...], rkw_ref[...], lnw_ref[...], lnb_ref[...], bones, lane_head)
    refs = (r_ref, k_ref, v_ref, ld_ref, a_ref, g_ref, y_ref)

    zero = jnp.zeros((N, N), F32)
    blocks = []
    for h in range(G):
        blocks.append(jnp.concatenate([s0_ref[0, h].T if j == h else zero for j in range(G)], axis=1))
    st_scr[...] = jnp.concatenate(blocks, axis=0)

    head = T % WKV_CHUNK
    if head:
        _wkv_chunk(0, head, refs, consts, st_scr)

    def body(i, carry):
        _wkv_chunk(pl.multiple_of(head + i * WKV_CHUNK, 16), WKV_CHUNK, refs, consts, st_scr)
        return carry

    lax.fori_loop(0, T // WKV_CHUNK, body, 0)

    st = st_scr[...]
    for h in range(G):
        sout_ref[0, h] = st[h * N:(h + 1) * N, h * N:(h + 1) * N].T


def _wkv(r, k, v, ld, a, g, kkw, kaw, rkw, lnw, lnb, s0):
    B, T, D = r.shape
    H = D // HEAD_A
    LW, G = WKV_LANES, WKV_GROUP
    tok = pl.BlockSpec((1, T, LW), lambda b, j: (b, 0, j))
    vec = pl.BlockSpec((1, LW), lambda b, j: (0, j))
    st = pl.BlockSpec((1, G, HEAD_A, HEAD_A), lambda b, j: (b, j, 0, 0))
    y, s_out = pl.pallas_call(
        functools.partial(_wkv_kernel, T=T),
        grid=(B, D // LW),
        in_specs=[tok] * 6 + [vec] * 5 + [st],
        out_specs=[tok, st],
        out_shape=[jax.ShapeDtypeStruct((B, T, D), BF16), jax.ShapeDtypeStruct(s0.shape, F32)],
        scratch_shapes=[pltpu.VMEM((LW, LW), F32)],
        compiler_params=_cparams(("parallel", "parallel")),
        name="wkv7",
    )(r, k, v, ld, a, g, *[p.reshape(1, D) for p in (kkw, kaw, rkw, lnw, lnb)], s0)
    return y, s_out


def _sb_pair(q, kt, vt, carry, acc, diag):
    nq, nk = q.shape[0], kt.shape[0]
    z = _bdot_nt(q, kt) * (HEAD_B ** -0.5)
    lk = -jax.nn.softplus(z)
    ji = lax.broadcasted_iota(jnp.int32, (nk, nk), 0)
    si = lax.broadcasted_iota(jnp.int32, (nk, nk), 1)
    later = jnp.where(ji > si, 1.0, 0.0).astype(BF16)
    if diag:
        read = lax.broadcasted_iota(jnp.int32, (nq, nk), 1) < lax.broadcasted_iota(jnp.int32, (nq, nk), 0)
        lkm = jnp.where(read, lk, 0.0)
    else:
        lkm = lk
    after = _split_dot(lkm, later, 3) + carry
    w = jnp.exp(z + lk + after)
    if diag:
        w = jnp.where(read, w, 0.0)
    acc = acc + _bdot(w, vt)
    carry = carry + jnp.sum(lkm, axis=1, keepdims=True)
    return carry, acc


def _attn_kernel(*refs, T, n_past):
    if n_past:
        q_ref, kn_ref, vn_ref, kp_ref, vp_ref, o_ref = refs
    else:
        q_ref, kn_ref, vn_ref, o_ref = refs
    TB = ATT_BLOCK
    n_full, rem = T // TB, T % TB

    def sweep(q, carry, acc, n_new):
        def new_body(j, ca):
            rows = pl.ds(pl.multiple_of((n_new - 1 - j) * TB, TB), TB)
            return _sb_pair(q, kn_ref[0, rows, :], vn_ref[0, rows, :], ca[0], ca[1], False)

        carry, acc = lax.fori_loop(0, n_new, new_body, (carry, acc))
        if n_past:
            def past_body(j, ca):
                rows = pl.ds(pl.multiple_of((n_past - 1 - j) * TB, TB), TB)
                return _sb_pair(q, kp_ref[0, rows, :], vp_ref[0, rows, :], ca[0], ca[1], False)

            carry, acc = lax.fori_loop(0, n_past, past_body, (carry, acc))
        return acc

    def full_tile(i, c):
        rows = pl.ds(pl.multiple_of(i * TB, TB), TB)
        q = q_ref[0, rows, :]
        carry, acc = _sb_pair(q, kn_ref[0, rows, :], vn_ref[0, rows, :],
                              jnp.zeros((TB, 1), F32), jnp.zeros((TB, HEAD_B), F32), True)
        o_ref[0, rows, :] = sweep(q, carry, acc, i).astype(o_ref.dtype)
        return c

    if n_full:
        lax.fori_loop(0, n_full, full_tile, 0)
    if rem:
        rows = pl.ds(n_full * TB, rem)
        q = q_ref[0, rows, :]
        carry, acc = _sb_pair(q, kn_ref[0, rows, :], vn_ref[0, rows, :],
                              jnp.zeros((rem, 1), F32), jnp.zeros((rem, HEAD_B), F32), True)
        o_ref[0, rows, :] = sweep(q, carry, acc, n_full).astype(o_ref.dtype)


def _attention(q, k_new, v_new, k_past, v_past):
    B, T, D = q.shape
    P = 0 if k_past is None else k_past.shape[1]
    assert P % ATT_BLOCK == 0
    new = pl.BlockSpec((1, T, HEAD_B), lambda b, h: (b, 0, h))
    in_specs = [new, new, new]
    args = [q, k_new, v_new]
    if P:
        past = pl.BlockSpec((1, P, HEAD_B), lambda b, h: (b, 0, h))
        in_specs += [past, past]
        args += [k_past, v_past]
    return pl.pallas_call(
        functools.partial(_attn_kernel, T=T, n_past=P // ATT_BLOCK),
        grid=(B, D // HEAD_B),
        in_specs=in_specs,
        out_specs=new,
        out_shape=jax.ShapeDtypeStruct((B, T, D), BF16),
        compiler_params=_cparams(("parallel", "parallel")),
        name="sb_attention",
    )(*args)


def _ffn_up_kernel(x_ref, wg_ref, wu_ref, st_ref, cw_ref, cb_ref, act_ref, nst_ref, *, T, spt):
    tm, tn = act_ref.shape
    x = x_ref[...]
    gate = jnp.dot(x, wg_ref[...], preferred_element_type=F32)
    up = jnp.dot(x, wu_ref[...], preferred_element_type=F32)
    st = st_ref[...]
    s0 = jnp.broadcast_to(st[:, 0:1, :], (spt, T, tn)).reshape(tm, tn)
    s1 = jnp.broadcast_to(st[:, 1:2, :], (spt, T, tn)).reshape(tm, tn)
    pos = lax.broadcasted_iota(jnp.int32, (tm, 1), 0) % T
    g1 = jnp.where(pos == 0, s1, pltpu.roll(gate, 1, axis=0))
    g2 = jnp.where(pos == 0, s0, jnp.where(pos == 1, s1, pltpu.roll(gate, 2, axis=0)))
    cw = cw_ref[...]
    conv = g2 * cw[0:1, :] + g1 * cw[1:2, :] + gate * cw[2:3, :] + cb_ref[...]
    act_ref[...] = (jax.nn.silu(conv) * up).astype(act_ref.dtype)
    nst_ref[...] = gate.reshape(spt, T, tn)[:, T - 2:T, :]


def _ffn_up(x, w_up, conv_state, conv_w, conv_b, T):
    M, D = x.shape
    F = w_up.shape[1] // 2
    n_seq = M // T
    spt = _pick(n_seq, (8, 4, 2, 1)) if T * 8 <= 512 else 1
    tm = spt * T
    tn = _pick(F, (256, 128))
    nf = F // tn
    x_mode = pl.Buffered(1) if tm * D * 2 > 8 * 1024 * 1024 else None
    st_spec = pl.BlockSpec((spt, 2, tn), lambda i, j: (i, 0, j))
    return pl.pallas_call(
        functools.partial(_ffn_up_kernel, T=T, spt=spt),
        grid=(M // tm, nf),
        in_specs=[pl.BlockSpec((tm, D), lambda i, j: (i, 0), pipeline_mode=x_mode),
                  pl.BlockSpec((D, tn), lambda i, j: (0, j)),
                  pl.BlockSpec((D, tn), lambda i, j: (0, j + nf)),
                  st_spec,
                  pl.BlockSpec((CONV_W, tn), lambda i, j: (0, j)),
                  pl.BlockSpec((1, tn), lambda i, j: (0, j))],
        out_specs=[pl.BlockSpec((tm, tn), lambda i, j: (i, j)), st_spec],
        out_shape=[jax.ShapeDtypeStruct((M, F), BF16), jax.ShapeDtypeStruct((n_seq, 2, F), F32)],
        compiler_params=_cparams(("parallel", "arbitrary")),
        name="ffn_up_conv",
    )(x, w_up, w_up, conv_state, conv_w, conv_b.reshape(1, F))


def _run_group(h, shift0, wkv0, conv0, past_k, past_v, p, depth, n_a):
    B, T, D = h.shape
    M = B * T
    flat = lambda t: t.reshape(M, D)
    seq = lambda t: t.reshape(B, T, D)
    gains = p['norm_gains']
    new_wkv, new_shift, new_conv = [], [], []
    v_first = None
    k_new = v_new = None
    h = flat(h)
    xin = None
    xkv = None
    for layer in range(depth):
        g = gains[layer]
        if layer < n_a:
            i = layer
            mixed_in, last_row = _normmix(seq(h), g[0], shift0[i], p['a_mix'][i])
            xr, xw, xk, xv, xa, xg = (flat(t) for t in mixed_in)
            new_shift.append(last_row)
            r = _matmul(xr, p['a_w_rkv'][i, 0], out_dtype=F32, name="rwkv_r")
            k = _matmul(xk, p['a_w_rkv'][i, 1], out_dtype=F32, name="rwkv_k")
            v = _matmul(xv, p['a_w_rkv'][i, 2], out_dtype=F32, name="rwkv_v")
            ld = _matmul(_matmul(xw, p['a_w1'][i], out_dtype=BF16, epilogue=_epi_tanh, name="lora_w1"),
                         p['a_w2'][i], out_dtype=F32, epilogue=_epi_log_decay,
                         rows=(p['a_w0'][i].reshape(1, D),), name="lora_w2")
            a = _matmul(_matmul(xa, p['a_a1'][i], out_dtype=BF16, name="lora_a1"),
                        p['a_a2'][i], out_dtype=F32, epilogue=_epi_bias_sigmoid,
                        rows=(p['a_a0'][i].reshape(1, D),), name="lora_a2")
            gate = _matmul(_matmul(xg, p['a_g1'][i], out_dtype=BF16, epilogue=_epi_sigmoid, name="lora_g1"),
                           p['a_g2'][i], out_dtype=F32, name="lora_g2")
            if i == 0:
                v_first = v
            else:
                v = _matmul(_matmul(xv, p['a_v1'][i - 1], out_dtype=BF16, name="lora_v1"),
                            p['a_v2'][i - 1], out_dtype=F32, epilogue=_epi_value_mix,
                            rows=(p['a_v0'][i - 1].reshape(1, D),), tiles=(v, v_first), name="lora_v2")
            y, s_out = _wkv(seq(r), seq(k), seq(v), seq(ld), seq(a), seq(gate),
                            p['a_k_k'][i], p['a_k_a'][i], p['a_r_k'][i].reshape(D), p['a_ln_w'][i],
                            p['a_ln_b'][i], wkv0[i])
            new_wkv.append(s_out)
            mixed = _matmul(flat(y), p['a_w_out'][i], out_dtype=F32, name="rwkv_out")
        else:
            j = layer - n_a
            if layer == n_a:
                kv = _matmul(xkv, p['w_kv'], out_dtype=F32, name="kv_proj")
                k_new, v_new = seq(kv[:, :D]), seq(kv[:, D:])
            q = _matmul(xin, p['b_w_q'][j], out_dtype=BF16, name="attn_q")
            attn = _attention(seq(q), k_new, v_new, past_k, past_v)
            mixed = _matmul(flat(attn), p['b_w_out'][j], out_dtype=F32, name="attn_out")
        h, (xf,) = _resnorm(h, mixed, g[1], [g[2]])
        act, conv_new = _ffn_up(xf, p['ffn_w_up'][layer], conv0[layer], p['ffn_conv_w'][layer],
                                p['ffn_conv_b'][layer], T)
        new_conv.append(conv_new)
        f = _matmul(act, p['ffn_w_down'][layer], out_dtype=F32, name="ffn_down")
        nxt = []
        if layer + 1 < depth and layer + 1 >= n_a:
            nxt.append(gains[layer + 1][0])
            if layer + 1 == n_a:
                nxt.append(p['kv_norm'])
        h, normed = _resnorm(h, f, g[3], nxt)
        if normed:
            xin = normed[0]
            if len(normed) > 1:
                xkv = normed[1]
    return seq(h), k_new, v_new, jnp.stack(new_wkv), jnp.stack(new_shift), jnp.stack(new_conv)


def kernel(x_prompt, x_sample, cache_k, cache_v, state_wkv, state_shift, state_ffn_conv, meta_tokens, norm_gains, a_mix, a_w_rkv, a_w_out, a_w0, a_w1, a_w2, a_a0, a_a1, a_a2, a_v0, a_v1, a_v2, a_g1, a_g2, a_k_k, a_k_a, a_r_k, a_ln_w, a_ln_b, kv_norm, w_kv, b_w_q, b_w_out, ffn_w_up, ffn_conv_w, ffn_conv_b, ffn_w_down):
    B, S, D = x_prompt.shape
    Bs, Ts, _ = x_sample.shape
    depth = norm_gains.shape[0]
    n_a = a_mix.shape[0]
    F = ffn_w_down.shape[1]
    H_a, H_b = D // HEAD_A, D // HEAD_B
    assert n_a >= 1 and depth > n_a
    bf = lambda w: w.astype(BF16)
    p = {
        'norm_gains': norm_gains, 'a_mix': a_mix, 'a_w_rkv': bf(a_w_rkv), 'a_w_out': bf(a_w_out),
        'a_w0': a_w0, 'a_w1': bf(a_w1), 'a_w2': bf(a_w2), 'a_a0': a_a0, 'a_a1': bf(a_a1),
        'a_a2': bf(a_a2), 'a_v0': a_v0, 'a_v1': bf(a_v1), 'a_v2': bf(a_v2), 'a_g1': bf(a_g1),
        'a_g2': bf(a_g2), 'a_k_k': a_k_k, 'a_k_a': a_k_a, 'a_r_k': a_r_k, 'a_ln_w': a_ln_w,
        'a_ln_b': a_ln_b, 'kv_norm': kv_norm, 'w_kv': bf(w_kv), 'b_w_q': bf(b_w_q),
        'b_w_out': bf(b_w_out), 'ffn_w_up': bf(ffn_w_up), 'ffn_conv_w': ffn_conv_w,
        'ffn_conv_b': ffn_conv_b, 'ffn_w_down': bf(ffn_w_down),
    }
    dt = x_prompt.dtype
    T = N_META + S
    meta = jnp.broadcast_to(meta_tokens[None].astype(dt), (B, N_META, D))
    h0 = jnp.concatenate([meta, x_prompt], axis=1)
    h_p, k_p, v_p, wkv_p, shift_p, conv_p = _run_group(
        h0, jnp.zeros((n_a, B, D), dt), jnp.zeros((n_a, B, H_a, HEAD_A, HEAD_A), dt),
        jnp.zeros((depth, B, CONV_W - 1, F), dt), None, None, p, depth, n_a)
    P = cache_k.shape[1]
    h_s, k_s, v_s, wkv_s, shift_s, conv_s = _run_group(
        x_sample, state_shift, state_wkv, state_ffn_conv, cache_k.reshape(Bs, P, D),
        cache_v.reshape(Bs, P, D), p, depth, n_a)
    heads = lambda t: t.reshape(t.shape[0], t.shape[1], H_b, HEAD_B)
    return (h_p[:, N_META:], h_s, heads(k_p), heads(v_p), heads(k_s), heads(v_s),
            wkv_p, wkv_s, shift_p, shift_s, conv_p, conv_s)
```

```python
import functools

import jax
import jax.numpy as jnp
from jax import lax
from jax.experimental import pallas as pl
from jax.experimental.pallas import tpu as pltpu

F32 = jnp.float32
BF16 = jnp.bfloat16

N_META = 16
HEAD_A = 64
HEAD_B = 128
GN_EPS = 64e-5
RMS_EPS = 1e-6
CONV_W = 3

V7X_LANES = 128
V7X_MXU_DIM = 256
V7X_VMEM_BYTES = 64 * 1024 * 1024
VMEM_LIMIT = V7X_VMEM_BYTES - 8 * 1024 * 1024

WKV_GROUP = 2
WKV_LANES = WKV_GROUP * HEAD_A
WKV_CHUNK = 64
ATT_TQ = 128
ATT_HEADS = 8


def _pick(n, prefs):
    for p in prefs:
        if p <= n and n % p == 0:
            return p
    return n


def _cparams(sem):
    return pltpu.CompilerParams(dimension_semantics=sem, vmem_limit_bytes=VMEM_LIMIT)


def _bdot(a, b):
    return jnp.dot(a.astype(BF16), b.astype(BF16), preferred_element_type=F32)


def _bdot_nt(a, b):
    return lax.dot_general(a.astype(BF16), b.astype(BF16), (((1,), (1,)), ((), ())),
                           preferred_element_type=F32)


def _bdot_tn(a, b):
    return lax.dot_general(a.astype(BF16), b.astype(BF16), (((0,), (0,)), ((), ())),
                           preferred_element_type=F32)


def _split(x, parts):
    out = []
    rem = x
    for i in range(parts):
        hi = rem.astype(BF16)
        out.append(hi)
        if i + 1 < parts:
            rem = rem - hi.astype(F32)
    return out


def _split_dot(x, m_bf16, parts):
    acc = None
    for hi in _split(x, parts):
        t = jnp.dot(hi, m_bf16, preferred_element_type=F32)
        acc = t if acc is None else acc + t
    return acc


def _split_dot_lhs(m_bf16, x, parts):
    acc = None
    for hi in _split(x, parts):
        t = jnp.dot(m_bf16, hi, preferred_element_type=F32)
        acc = t if acc is None else acc + t
    return acc


def _interleave(gens):
    results = [None] * len(gens)
    live = list(range(len(gens)))
    while live:
        still = []
        for i in live:
            try:
                next(gens[i])
                still.append(i)
            except StopIteration as stop:
                results[i] = stop.value
        live = still
    return results


def _rms(x, g):
    return x * lax.rsqrt(jnp.mean(x * x, axis=-1, keepdims=True) + RMS_EPS) * g


def _epi_none(acc):
    return acc


def _epi_tanh(acc):
    return jnp.tanh(acc)


def _epi_sigmoid(acc):
    return jax.nn.sigmoid(acc)


def _epi_bias_sigmoid(acc, bias):
    return jax.nn.sigmoid(bias + acc)


def _epi_log_decay(acc, w0):
    return -jnp.exp(-jax.nn.softplus(-(w0 + acc)) - 0.5)


def _epi_value_mix(acc, v0, v, v_first):
    return v + (v_first - v) * jax.nn.sigmoid(v0 + acc)


def _mm_kernel(*refs, epilogue, n_row, n_tile):
    x_ref, w_ref = refs[0], refs[1]
    rows = [r[...] for r in refs[2:2 + n_row]]
    tiles = [r[...] for r in refs[2 + n_row:2 + n_row + n_tile]]
    acc = jnp.dot(x_ref[...], w_ref[...], preferred_element_type=F32)
    res = epilogue(acc, *rows, *tiles)
    for o_ref in refs[2 + n_row + n_tile:]:
        o_ref[...] = res.astype(o_ref.dtype)


def _matmul(x, w, *, out_dtype, epilogue=_epi_none, rows=(), tiles=(), n=None, col_off=0, name):
    M, K = x.shape
    N = w.shape[1] if n is None else n
    dtypes = out_dtype if isinstance(out_dtype, tuple) else (out_dtype,)
    tm = _pick(M, (2064, 1376, 688, 512, 256, 128, 64, 16))
    if K > 8192:
        tm = _pick(M, (688, 512, 256, 128, 64, 16))
    tn = _pick(N, (512, 256, 128))
    assert col_off % tn == 0
    joff = col_off // tn
    x_mode = pl.Buffered(1) if tm * K * 2 > 8 * 1024 * 1024 else None
    in_specs = [pl.BlockSpec((tm, K), lambda i, j: (i, 0), pipeline_mode=x_mode),
                pl.BlockSpec((K, tn), lambda i, j: (0, j + joff))]
    in_specs += [pl.BlockSpec((1, tn), lambda i, j: (0, j)) for _ in rows]
    in_specs += [pl.BlockSpec((tm, tn), lambda i, j: (i, j)) for _ in tiles]
    outs = pl.pallas_call(
        functools.partial(_mm_kernel, epilogue=epilogue, n_row=len(rows), n_tile=len(tiles)),
        grid=(M // tm, N // tn),
        in_specs=in_specs,
        out_specs=[pl.BlockSpec((tm, tn), lambda i, j: (i, j)) for _ in dtypes],
        out_shape=[jax.ShapeDtypeStruct((M, N), d) for d in dtypes],
        compiler_params=_cparams(("parallel", "arbitrary")),
        name=name,
    )(x, w, *rows, *tiles)
    return outs if isinstance(out_dtype, tuple) else outs[0]


def _resnorm_kernel(*refs, n_next):
    h_ref, y_ref, gp_ref = refs[0], refs[1], refs[2]
    gn_refs = refs[3:3 + n_next]
    hn_ref = refs[3 + n_next]
    xo_refs = refs[4 + n_next:]
    hn = h_ref[...] + _rms(y_ref[...], gp_ref[...])
    hn_ref[...] = hn
    for gn_ref, xo_ref in zip(gn_refs, xo_refs):
        xo_ref[...] = _rms(hn, gn_ref[...]).astype(xo_ref.dtype)


def _resnorm(h, y, g_post, g_next):
    M, D = h.shape
    tr = _pick(M, (192, 128, 64, 16))
    n = len(g_next)
    row = pl.BlockSpec((tr, D), lambda i: (i, 0))
    vec = pl.BlockSpec((1, D), lambda i: (0, 0))
    outs = pl.pallas_call(
        functools.partial(_resnorm_kernel, n_next=n),
        grid=(M // tr,),
        in_specs=[row, row, vec] + [vec] * n,
        out_specs=[row] + [row] * n,
        out_shape=[jax.ShapeDtypeStruct((M, D), F32)] + [jax.ShapeDtypeStruct((M, D), BF16)] * n,
        compiler_params=_cparams(("parallel",)),
        name="resnorm",
    )(h, y, g_post.reshape(1, D), *[g.reshape(1, D) for g in g_next])
    return outs[0], list(outs[1:])


def _normmix_kernel(h_ref, halo_ref, g_ref, shift_ref, mix_ref, *out_refs, tt):
    t = pl.program_id(1)
    g = g_ref[...]
    xn = _rms(h_ref[0], g)
    halo = _rms(halo_ref[0], g)
    prev_row = jnp.where(t == 0, shift_ref[0], halo[7:8, :])
    row = lax.broadcasted_iota(jnp.int32, (tt, 1), 0)
    x_prev = jnp.where(row == 0, prev_row, pltpu.roll(xn, 1, axis=0))
    xx = x_prev - xn
    for i in range(6):
        out_refs[i][0] = (xn + xx * mix_ref[i:i + 1, :]).astype(BF16)
    out_refs[6][0] = xn[tt - 1:tt, :]


def _normmix(h, g, shift0, mix):
    B, T, D = h.shape
    tt = _pick(T, (64, 48, 32, 16))
    nb = tt // 8
    tok = pl.BlockSpec((1, tt, D), lambda b, t: (b, t, 0))
    outs = pl.pallas_call(
        functools.partial(_normmix_kernel, tt=tt),
        grid=(B, T // tt),
        in_specs=[tok,
                  pl.BlockSpec((1, 8, D), lambda b, t: (b, jnp.maximum(t * nb - 1, 0), 0)),
                  pl.BlockSpec((1, D), lambda b, t: (0, 0)),
                  pl.BlockSpec((1, 1, D), lambda b, t: (b, 0, 0)),
                  pl.BlockSpec((6, D), lambda b, t: (0, 0))],
        out_specs=[tok] * 6 + [pl.BlockSpec((1, 1, D), lambda b, t: (b, 0, 0))],
        out_shape=[jax.ShapeDtypeStruct((B, T, D), BF16)] * 6 + [jax.ShapeDtypeStruct((B, 1, D), F32)],
        compiler_params=_cparams(("parallel", "arbitrary")),
        name="normmix",
    )(h, h, g.reshape(1, D), shift0.reshape(B, 1, D), mix)
    return list(outs[:6]), outs[6].reshape(B, D)


def _wkv_phase1(c0, n_valid, lanes, refs, prm, bones):
    r_ref, k_ref, v_ref, ld_ref, a_ref = refs
    kkw, kaw, rkw = prm
    G, LW, C = WKV_GROUP, WKV_LANES, WKV_CHUNK
    R = G * C
    sl = pl.ds(c0, C)
    r = r_ref[0, sl, lanes]
    k = k_ref[0, sl, lanes]
    v = v_ref[0, sl, lanes]
    ld = ld_ref[0, sl, lanes]
    a = a_ref[0, sl, lanes]
    if n_valid < C:
        valid = lax.broadcasted_iota(jnp.int32, (C, 1), 0) < n_valid
        r, k, v, ld = (jnp.where(valid, x, 0.0) for x in (r, k, v, ld))

    kk = k * kkw
    km = k * (1.0 + (a - 1.0) * kaw)
    ti = lax.broadcasted_iota(jnp.int32, (C, C), 0)
    si = lax.broadcasted_iota(jnp.int32, (C, C), 1)
    kk_ss = _split_dot(kk * kk, bones, 2)
    rk_sum = _split_dot(r * km * rkw, bones, 2)
    cum = _split_dot_lhs(jnp.where(ti >= si, 1.0, 0.0).astype(BF16), ld, 3)
    yield
    kk = kk / jnp.maximum(jnp.sqrt(kk_ss), 1e-12)
    bonus = rk_sum * v
    b = kk * a
    cum_last = cum[C - 1:C, :]
    e_neg = jnp.exp(-cum)
    e_rem = jnp.exp(cum_last - cum)
    lane_head = lax.broadcasted_iota(jnp.int32, (1, LW), 1) // HEAD_A

    def stack(x):
        return jnp.concatenate([jnp.where(lane_head == h, x, 0.0) for h in range(G)], axis=0)

    r_s = stack(jnp.exp(cum) * r)
    a_b = stack(jnp.exp(cum - ld) * (-kk)).astype(BF16)
    v_b = stack(v).astype(BF16)
    ar_b = jnp.concatenate([a_b, r_s.astype(BF16)], axis=0)
    bk_b = jnp.concatenate([stack(e_neg * b), stack(e_neg * km)], axis=0).astype(BF16)
    bkh_t = jnp.concatenate([stack(e_rem * b).T, stack(e_rem * km).T], axis=1).astype(BF16)

    big = _bdot_nt(ar_b, bk_b)
    yield
    row = lax.broadcasted_iota(jnp.int32, (R, R), 0)
    col = lax.broadcasted_iota(jnp.int32, (R, R), 1)
    same = (row // C) == (col // C)
    strict = same & (row > col)
    incl = same & (row >= col)
    n_ab = jnp.where(strict, big[:R, :R], 0.0)
    a_ak = jnp.where(strict, big[:R, R:], 0.0)
    a_rb = jnp.where(incl, big[R:, :R], 0.0).astype(BF16)
    a_rk = jnp.where(incl, big[R:, R:], 0.0)
    av = _bdot(jnp.concatenate([a_ak, a_rk], axis=0), v_b)

    tinv = jnp.where(row == col, 1.0, 0.0) + n_ab
    npow = n_ab
    span = 2
    while span < C:
        npow_b = npow.astype(BF16)
        new_pow = _bdot(npow_b, npow_b)
        if span > 2:
            tinv = tinv + _bdot(tinv, npow_b)
        npow = new_pow
        span *= 2
        yield
    tinv = tinv + _bdot(tinv, npow)
    yield

    wu_b = _bdot(tinv, jnp.concatenate([a_b, av[:R].astype(BF16)], axis=1)).astype(BF16)
    yield
    qo = _bdot(a_rb, wu_b)
    low = jnp.concatenate([jnp.zeros((R, LW), BF16), v_b], axis=1)
    m = _bdot(bkh_t, jnp.concatenate([wu_b, low], axis=0))
    yield
    q_b = (r_s + qo[:, :LW]).astype(BF16)
    o0 = qo[:, LW:] + av[R:]
    li = lax.broadcasted_iota(jnp.int32, (LW, LW), 0)
    lj = lax.broadcasted_iota(jnp.int32, (LW, LW), 1)
    m1_b = (jnp.where(li == lj, jnp.exp(cum_last), 0.0) + m[:, :LW]).astype(BF16)
    return jnp.concatenate([q_b, m1_b], axis=0), o0, m[:, LW:], bonus


def _wkv_phase2(state, ph, c0, n_valid, lanes, g_ref, y_ref, lnw, lnb, bones):
    qm_b, o0, m2, bonus = ph
    C = WKV_CHUNK
    R = WKV_GROUP * C
    qs = _bdot(qm_b, state)
    yield
    o_st = qs[:R] + o0
    new_state = qs[R:] + m2
    o = o_st[0:C, :]
    for h in range(1, WKV_GROUP):
        o = o + o_st[h * C:(h + 1) * C, :]
    mu = _split_dot(o, bones, 2) * (1.0 / HEAD_A)
    yield
    d = o - mu
    var = _split_dot(d * d, bones, 2) * (1.0 / HEAD_A)
    yield
    o = d * lax.rsqrt(var + GN_EPS) * lnw + lnb + bonus
    sl = pl.ds(c0, n_valid)
    y_ref[0, sl, lanes] = (o[:n_valid] * g_ref[0, sl, lanes]).astype(y_ref.dtype)
    return new_state


def _wkv_kernel(r_ref, k_ref, v_ref, ld_ref, a_ref, g_ref, kkw_ref, kaw_ref, rkw_ref, lnw_ref, lnb_ref,
                s0_ref, y_ref, sout_ref, *, T, n_tiles, n_par):
    G, LW, N = WKV_GROUP, WKV_LANES, HEAD_A
    li = lax.broadcasted_iota(jnp.int32, (LW, LW), 0)
    lj = lax.broadcasted_iota(jnp.int32, (LW, LW), 1)
    bones = jnp.where((li // N) == (lj // N), 1.0, 0.0).astype(BF16)
    refs = (r_ref, k_ref, v_ref, ld_ref, a_ref)
    lane_sl = [slice(p * LW, (p + 1) * LW) for p in range(n_tiles)]
    prm = [(kkw_ref[:, s], kaw_ref[:, s], rkw_ref[:, s]) for s in lane_sl]
    post = [(lnw_ref[:, s], lnb_ref[:, s]) for s in lane_sl]

    def steps(c_base, n_valid, n_chunks, states):
        C = WKV_CHUNK
        ph = _interleave([_wkv_phase1(c_base + u * C, n_valid, lane_sl[p], refs, prm[p], bones)
                          for u in range(n_chunks) for p in range(n_tiles)])
        states = list(states)
        for u in range(n_chunks):
            states = _interleave([
                _wkv_phase2(states[p].astype(BF16), ph[u * n_tiles + p], c_base + u * C, n_valid, lane_sl[p],
                            g_ref, y_ref, post[p][0], post[p][1], bones) for p in range(n_tiles)])
        return tuple(states)

    zero = jnp.zeros((N, N), F32)
    states = []
    for p in range(n_tiles):
        blocks = [jnp.concatenate([s0_ref[0, p * G + h].T if j == h else zero for j in range(G)], axis=1)
                  for h in range(G)]
        states.append(jnp.concatenate(blocks, axis=0))
    states = tuple(states)

    head = T % WKV_CHUNK
    if head:
        states = steps(0, head, 1, states)
    n_main = T // WKV_CHUNK
    if n_main:
        def body(i, st):
            return steps(pl.multiple_of(head + i * (n_par * WKV_CHUNK), 16), WKV_CHUNK, n_par, st)

        states = lax.fori_loop(0, n_main // n_par, body, states)

    for p in range(n_tiles):
        for h in range(G):
            sout_ref[0, p * G + h] = states[p][h * N:(h + 1) * N, h * N:(h + 1) * N].T


def _wkv(r, k, v, ld, a, g, kkw, kaw, rkw, lnw, lnb, s0):
    B, T, D = r.shape
    LW, G = WKV_LANES, WKV_GROUP
    n_main = T // WKV_CHUNK
    assert T >= WKV_CHUNK
    n_par = 2 if n_main % 2 == 0 else 1
    n_tiles = _pick(D // LW, (2,)) if n_main > 1 else _pick(D // LW, (4, 2))
    bw = n_tiles * LW
    tok = pl.BlockSpec((1, T, bw), lambda b, j: (b, 0, j))
    vec = pl.BlockSpec((1, bw), lambda b, j: (0, j))
    st = pl.BlockSpec((1, n_tiles * G, HEAD_A, HEAD_A), lambda b, j: (b, j, 0, 0))
    y, s_out = pl.pallas_call(
        functools.partial(_wkv_kernel, T=T, n_tiles=n_tiles, n_par=n_par),
        grid=(B, D // bw),
        in_specs=[tok] * 6 + [vec] * 5 + [st],
        out_specs=[tok, st],
        out_shape=[jax.ShapeDtypeStruct((B, T, D), BF16), jax.ShapeDtypeStruct(s0.shape, F32)],
        compiler_params=_cparams(("parallel", "parallel")),
        name="wkv7",
    )(r, k, v, ld, a, g, *[p.reshape(1, D) for p in (kkw, kaw, rkw, lnw, lnb)], s0)
    return y, s_out


def _sb_consts(nk):
    ji = lax.broadcasted_iota(jnp.int32, (nk, nk), 0)
    si = lax.broadcasted_iota(jnp.int32, (nk, nk), 1)
    later = jnp.where(ji > si, 1.0, 0.0).astype(BF16)
    return jnp.concatenate([later, later], axis=0)


def _sb_pair(q, kt, vt, carry, acc, later2, read):
    nq, nk = q.shape[0], kt.shape[0]
    z = _bdot_nt(q, kt) * (HEAD_B ** -0.5)
    yield
    lk = -(jnp.maximum(z, 0.0) + jnp.log(1.0 + jnp.exp(-jnp.abs(z))))
    lkm = lk if read is None else jnp.where(read, lk, 0.0)
    hi, lo = _split(lkm, 2)
    if nk % V7X_LANES == 0:
        local = jnp.dot(jnp.concatenate([hi, lo], axis=1), later2, preferred_element_type=F32)
    else:
        local = (jnp.dot(hi, later2[:nk], preferred_element_type=F32)
                 + jnp.dot(lo, later2[:nk], preferred_element_type=F32))
    yield
    total = jnp.broadcast_to(local[:, 0:1] + lkm[:, 0:1], (nq, V7X_LANES))
    if nk % V7X_LANES == 0:
        after = local + jnp.concatenate([carry] * (nk // V7X_LANES), axis=1)
    else:
        after = local + carry[:, :nk]
    w = jnp.exp(z + lk + after)
    if read is not None:
        w = jnp.where(read, w, 0.0)
    return carry + total, acc + _bdot(w, vt)


def _attn_kernel(*refs, T, P, KN, KP):
    if P:
        q_ref, kn_ref, vn_ref, kp_ref, vp_ref, o_ref = refs
    else:
        q_ref, kn_ref, vn_ref, o_ref = refs
    TQ, HB = ATT_TQ, HEAD_B
    n_heads = q_ref.shape[2] // HB
    lanes = [slice(h * HB, (h + 1) * HB) for h in range(n_heads)]
    n_full, rem = T // TQ, T % TQ
    c_new = _sb_consts(KN)
    c_past = _sb_consts(KP) if P else None

    def sweep(qs, ca, n_new):
        def step(k_ref, v_ref, width, consts, jj, ca):
            rows = pl.ds(pl.multiple_of(jj * width, width), width)
            return tuple(_interleave([_sb_pair(qs[h], k_ref[0, rows, lanes[h]], v_ref[0, rows, lanes[h]],
                                               ca[h][0], ca[h][1], consts, None) for h in range(n_heads)]))

        ca = lax.fori_loop(0, n_new, lambda j, c: step(kn_ref, vn_ref, KN, c_new, n_new - 1 - j, c), ca)
        if P:
            n_past = P // KP
            ca = lax.fori_loop(0, n_past, lambda j, c: step(kp_ref, vp_ref, KP, c_past, n_past - 1 - j, c), ca)
        return ca

    def query_tile(q0, nq, k0, nk, off, consts):
        qrows = pl.ds(q0, nq)
        krows = pl.ds(k0, nk)
        qs = [q_ref[0, qrows, lanes[h]] for h in range(n_heads)]
        cmr = lax.broadcasted_iota(jnp.int32, (nq, nk), 1) - lax.broadcasted_iota(jnp.int32, (nq, nk), 0)
        read = cmr < off
        zc = jnp.zeros((nq, V7X_LANES), F32)
        za = jnp.zeros((nq, HB), F32)
        ca = tuple(_interleave([_sb_pair(qs[h], kn_ref[0, krows, lanes[h]], vn_ref[0, krows, lanes[h]],
                                         zc, za, consts, read) for h in range(n_heads)]))
        ca = sweep(qs, ca, k0 // KN)
        for h in range(n_heads):
            o_ref[0, qrows, lanes[h]] = ca[h][1].astype(o_ref.dtype)

    if n_full:
        def full_tile(i, c):
            q0 = pl.multiple_of(i * TQ, TQ)
            k0 = pl.multiple_of((q0 // KN) * KN, KN)
            query_tile(q0, TQ, k0, KN, q0 - k0, c_new)
            return c

        lax.fori_loop(0, n_full, full_tile, 0)
    if rem:
        q0 = n_full * TQ
        query_tile(q0, rem, q0, rem, 0, _sb_consts(rem))


def _attention(q, k_new, v_new, k_past, v_past):
    B, T, D = q.shape
    P = 0 if k_past is None else k_past.shape[1]
    full = T - T % ATT_TQ
    KN = V7X_MXU_DIM if full % V7X_MXU_DIM == 0 and full else ATT_TQ
    KP = V7X_MXU_DIM if P % V7X_MXU_DIM == 0 else ATT_TQ
    assert P % KP == 0 and full % KN == 0
    bw = HEAD_B * _pick(D // HEAD_B, (ATT_HEADS, 2, 1))
    new = pl.BlockSpec((1, T, bw), lambda b, h: (b, 0, h))
    in_specs = [new, new, new]
    args = [q, k_new, v_new]
    if P:
        past = pl.BlockSpec((1, P, bw), lambda b, h: (b, 0, h))
        in_specs += [past, past]
        args += [k_past, v_past]
    return pl.pallas_call(
        functools.partial(_attn_kernel, T=T, P=P, KN=KN, KP=KP),
        grid=(B, D // bw),
        in_specs=in_specs,
        out_specs=new,
        out_shape=jax.ShapeDtypeStruct((B, T, D), BF16),
        compiler_params=_cparams(("parallel", "parallel")),
        name="sb_attention",
    )(*args)


def _ffn_up_kernel(x_ref, wg_ref, wu_ref, st_ref, cw_ref, cb_ref, act_ref, nst_ref, *, T, spt):
    tm, tn = act_ref.shape
    x = x_ref[...]
    gate = jnp.dot(x, wg_ref[...], preferred_element_type=F32)
    up = jnp.dot(x, wu_ref[...], preferred_element_type=F32)
    st = st_ref[...]
    s0 = jnp.broadcast_to(st[:, 0:1, :], (spt, T, tn)).reshape(tm, tn)
    s1 = jnp.broadcast_to(st[:, 1:2, :], (spt, T, tn)).reshape(tm, tn)
    pos = lax.broadcasted_iota(jnp.int32, (tm, 1), 0) % T
    g1 = jnp.where(pos == 0, s1, pltpu.roll(gate, 1, axis=0))
    g2 = jnp.where(pos == 0, s0, jnp.where(pos == 1, s1, pltpu.roll(gate, 2, axis=0)))
    cw = cw_ref[...]
    conv = g2 * cw[0:1, :] + g1 * cw[1:2, :] + gate * cw[2:3, :] + cb_ref[...]
    act_ref[...] = (jax.nn.silu(conv) * up).astype(act_ref.dtype)
    nst_ref[...] = gate.reshape(spt, T, tn)[:, T - 2:T, :]


def _ffn_up(x, w_up, conv_state, conv_w, conv_b, T):
    M, D = x.shape
    F = w_up.shape[1] // 2
    n_seq = M // T
    spt = _pick(n_seq, (8, 4, 2, 1)) if T * 8 <= 512 else 1
    tm = spt * T
    tn = _pick(F, (256, 128))
    nf = F // tn
    x_mode = pl.Buffered(1) if tm * D * 2 > 8 * 1024 * 1024 else None
    st_spec = pl.BlockSpec((spt, 2, tn), lambda i, j: (i, 0, j))
    return pl.pallas_call(
        functools.partial(_ffn_up_kernel, T=T, spt=spt),
        grid=(M // tm, nf),
        in_specs=[pl.BlockSpec((tm, D), lambda i, j: (i, 0), pipeline_mode=x_mode),
                  pl.BlockSpec((D, tn), lambda i, j: (0, j)),
                  pl.BlockSpec((D, tn), lambda i, j: (0, j + nf)),
                  st_spec,
                  pl.BlockSpec((CONV_W, tn), lambda i, j: (0, j)),
                  pl.BlockSpec((1, tn), lambda i, j: (0, j))],
        out_specs=[pl.BlockSpec((tm, tn), lambda i, j: (i, j)), st_spec],
        out_shape=[jax.ShapeDtypeStruct((M, F), BF16), jax.ShapeDtypeStruct((n_seq, 2, F), F32)],
        compiler_params=_cparams(("parallel", "arbitrary")),
        name="ffn_up_conv",
    )(x, w_up, w_up, conv_state, conv_w, conv_b.reshape(1, F))


def _run_group(h, shift0, wkv0, conv0, past_k, past_v, p, depth, n_a):
    B, T, D = h.shape
    M = B * T
    flat = lambda t: t.reshape(M, D)
    seq = lambda t: t.reshape(B, T, D)
    gains = p['norm_gains']
    new_wkv, new_shift, new_conv = [], [], []
    v_first = None
    k_new = v_new = k_bf = v_bf = None
    h = flat(h)
    xin = None
    xkv = None
    for layer in range(depth):
        g = gains[layer]
        if layer < n_a:
            i = layer
            mixed_in, last_row = _normmix(seq(h), g[0], shift0[i], p['a_mix'][i])
            xr, xw, xk, xv, xa, xg = (flat(t) for t in mixed_in)
            new_shift.append(last_row)
            r = _matmul(xr, p['a_w_rkv'][i, 0], out_dtype=F32, name="rwkv_r")
            k = _matmul(xk, p['a_w_rkv'][i, 1], out_dtype=F32, name="rwkv_k")
            v = _matmul(xv, p['a_w_rkv'][i, 2], out_dtype=F32, name="rwkv_v")
            ld = _matmul(_matmul(xw, p['a_w1'][i], out_dtype=BF16, epilogue=_epi_tanh, name="lora_w1"),
                         p['a_w2'][i], out_dtype=F32, epilogue=_epi_log_decay,
                         rows=(p['a_w0'][i].reshape(1, D),), name="lora_w2")
            a = _matmul(_matmul(xa, p['a_a1'][i], out_dtype=BF16, name="lora_a1"),
                        p['a_a2'][i], out_dtype=F32, epilogue=_epi_bias_sigmoid,
                        rows=(p['a_a0'][i].reshape(1, D),), name="lora_a2")
            gate = _matmul(_matmul(xg, p['a_g1'][i], out_dtype=BF16, epilogue=_epi_sigmoid, name="lora_g1"),
                           p['a_g2'][i], out_dtype=F32, name="lora_g2")
            if i == 0:
                v_first = v
            else:
                v = _matmul(_matmul(xv, p['a_v1'][i - 1], out_dtype=BF16, name="lora_v1"),
                            p['a_v2'][i - 1], out_dtype=F32, epilogue=_epi_value_mix,
                            rows=(p['a_v0'][i - 1].reshape(1, D),), tiles=(v, v_first), name="lora_v2")
            y, s_out = _wkv(seq(r), seq(k), seq(v), seq(ld), seq(a), seq(gate),
                            p['a_k_k'][i], p['a_k_a'][i], p['a_r_k'][i].reshape(D), p['a_ln_w'][i],
                            p['a_ln_b'][i], wkv0[i])
            new_wkv.append(s_out)
            mixed = _matmul(flat(y), p['a_w_out'][i], out_dtype=F32, name="rwkv_out")
        else:
            j = layer - n_a
            if layer == n_a:
                k_new, k_bf = _matmul(xkv, p['w_kv'], out_dtype=(F32, BF16), n=D, name="kv_proj_k")
                v_new, v_bf = _matmul(xkv, p['w_kv'], out_dtype=(F32, BF16), n=D, col_off=D, name="kv_proj_v")
            q = _matmul(xin, p['b_w_q'][j], out_dtype=BF16, name="attn_q")
            attn = _attention(seq(q), seq(k_bf), seq(v_bf), past_k, past_v)
            mixed = _matmul(flat(attn), p['b_w_out'][j], out_dtype=F32, name="attn_out")
        h, (xf,) = _resnorm(h, mixed, g[1], [g[2]])
        act, conv_new = _ffn_up(xf, p['ffn_w_up'][layer], conv0[layer], p['ffn_conv_w'][layer],
                                p['ffn_conv_b'][layer], T)
        new_conv.append(conv_new)
        f = _matmul(act, p['ffn_w_down'][layer], out_dtype=F32, name="ffn_down")
        nxt = []
        if layer + 1 < depth and layer + 1 >= n_a:
            nxt.append(gains[layer + 1][0])
            if layer + 1 == n_a:
                nxt.append(p['kv_norm'])
        h, normed = _resnorm(h, f, g[3], nxt)
        if normed:
            xin = normed[0]
            if len(normed) > 1:
                xkv = normed[1]
    return (seq(h), seq(k_new), seq(v_new), jnp.stack(new_wkv), jnp.stack(new_shift),
            jnp.stack(new_conv))


def kernel(x_prompt, x_sample, cache_k, cache_v, state_wkv, state_shift, state_ffn_conv, meta_tokens, norm_gains, a_mix, a_w_rkv, a_w_out, a_w0, a_w1, a_w2, a_a0, a_a1, a_a2, a_v0, a_v1, a_v2, a_g1, a_g2, a_k_k, a_k_a, a_r_k, a_ln_w, a_ln_b, kv_norm, w_kv, b_w_q, b_w_out, ffn_w_up, ffn_conv_w, ffn_conv_b, ffn_w_down):
    B, S, D = x_prompt.shape
    Bs, Ts, _ = x_sample.shape
    depth = norm_gains.shape[0]
    n_a = a_mix.shape[0]
    F = ffn_w_down.shape[1]
    H_a, H_b = D // HEAD_A, D // HEAD_B
    assert n_a >= 1 and depth > n_a
    bf = lambda w: w.astype(BF16)
    p = {
        'norm_gains': norm_gains, 'a_mix': a_mix, 'a_w_rkv': bf(a_w_rkv), 'a_w_out': bf(a_w_out),
        'a_w0': a_w0, 'a_w1': bf(a_w1), 'a_w2': bf(a_w2), 'a_a0': a_a0, 'a_a1': bf(a_a1),
        'a_a2': bf(a_a2), 'a_v0': a_v0, 'a_v1': bf(a_v1), 'a_v2': bf(a_v2), 'a_g1': bf(a_g1),
        'a_g2': bf(a_g2), 'a_k_k': a_k_k, 'a_k_a': a_k_a, 'a_r_k': a_r_k, 'a_ln_w': a_ln_w,
        'a_ln_b': a_ln_b, 'kv_norm': kv_norm, 'w_kv': bf(w_kv), 'b_w_q': bf(b_w_q),
        'b_w_out': bf(b_w_out), 'ffn_w_up': bf(ffn_w_up), 'ffn_conv_w': ffn_conv_w,
        'ffn_conv_b': ffn_conv_b, 'ffn_w_down': bf(ffn_w_down),
    }
    dt = x_prompt.dtype
    T = N_META + S
    meta = jnp.broadcast_to(meta_tokens[None].astype(dt), (B, N_META, D))
    h0 = jnp.concatenate([meta, x_prompt], axis=1)
    h_p, k_p, v_p, wkv_p, shift_p, conv_p = _run_group(
        h0, jnp.zeros((n_a, B, D), dt), jnp.zeros((n_a, B, H_a, HEAD_A, HEAD_A), dt),
        jnp.zeros((depth, B, CONV_W - 1, F), dt), None, None, p, depth, n_a)
    P = cache_k.shape[1]
    h_s, k_s, v_s, wkv_s, shift_s, conv_s = _run_group(
        x_sample, state_shift, state_wkv, state_ffn_conv, bf(cache_k).reshape(Bs, P, D),
        bf(cache_v).reshape(Bs, P, D), p, depth, n_a)
    heads = lambda t: t.reshape(t.shape[0], t.shape[1], H_b, HEAD_B)
    return (h_p[:, N_META:], h_s, heads(k_p), heads(v_p), heads(k_s), heads(v_s),
            wkv_p, wkv_s, shift_p, shift_s, conv_p, conv_s)
```

```python
import functools

import jax
import jax.numpy as jnp
from jax import lax
from jax.experimental import pallas as pl
from jax.experimental.pallas import tpu as pltpu

F32 = jnp.float32
BF16 = jnp.bfloat16

N_META = 16
HEAD_A = 64
HEAD_B = 128
GN_EPS = 64e-5
RMS_EPS = 1e-6
CONV_W = 3

V7X_LANES = 128
V7X_MXU_DIM = 256
V7X_VMEM_BYTES = 64 * 1024 * 1024
VMEM_LIMIT = V7X_VMEM_BYTES - 8 * 1024 * 1024

WKV_GROUP = 2
WKV_LANES = WKV_GROUP * HEAD_A
WKV_CHUNK = 64
ATT_TQ = 128
ATT_HEADS = 8


def _pick(n, prefs):
    for p in prefs:
        if p <= n and n % p == 0:
            return p
    return n


def _cparams(sem):
    return pltpu.CompilerParams(dimension_semantics=sem, vmem_limit_bytes=VMEM_LIMIT)


def _bdot(a, b):
    return jnp.dot(a.astype(BF16), b.astype(BF16), preferred_element_type=F32)


def _bdot_nt(a, b):
    return lax.dot_general(a.astype(BF16), b.astype(BF16), (((1,), (1,)), ((), ())),
                           preferred_element_type=F32)


def _bdot_tn(a, b):
    return lax.dot_general(a.astype(BF16), b.astype(BF16), (((0,), (0,)), ((), ())),
                           preferred_element_type=F32)


def _split(x, parts):
    out = []
    rem = x
    for i in range(parts):
        hi = rem.astype(BF16)
        out.append(hi)
        if i + 1 < parts:
            rem = rem - hi.astype(F32)
    return out


def _split_dot(x, m_bf16, parts):
    acc = None
    for hi in _split(x, parts):
        t = jnp.dot(hi, m_bf16, preferred_element_type=F32)
        acc = t if acc is None else acc + t
    return acc


def _split_dot_lhs(m_bf16, x, parts):
    acc = None
    for hi in _split(x, parts):
        t = jnp.dot(m_bf16, hi, preferred_element_type=F32)
        acc = t if acc is None else acc + t
    return acc


def _interleave(gens):
    results = [None] * len(gens)
    live = list(range(len(gens)))
    while live:
        still = []
        for i in live:
            try:
                next(gens[i])
                still.append(i)
            except StopIteration as stop:
                results[i] = stop.value
        live = still
    return results


def _rms(x, g):
    return x * lax.rsqrt(jnp.mean(x * x, axis=-1, keepdims=True) + RMS_EPS) * g


def _epi_none(acc):
    return acc


def _epi_tanh(acc):
    return jnp.tanh(acc)


def _epi_sigmoid(acc):
    return jax.nn.sigmoid(acc)


def _epi_bias_sigmoid(acc, bias):
    return jax.nn.sigmoid(bias + acc)


def _epi_log_decay(acc, w0):
    return -jnp.exp(-jax.nn.softplus(-(w0 + acc)) - 0.5)


def _epi_value_mix(acc, v0, v, v_first):
    return v + (v_first - v) * jax.nn.sigmoid(v0 + acc)


def _mm_kernel(*refs, epilogue, n_row, n_tile):
    x_ref, w_ref = refs[0], refs[1]
    rows = [r[...] for r in refs[2:2 + n_row]]
    tiles = [r[...] for r in refs[2 + n_row:2 + n_row + n_tile]]
    acc = jnp.dot(x_ref[...], w_ref[...].astype(BF16), preferred_element_type=F32)
    res = epilogue(acc, *rows, *tiles)
    for o_ref in refs[2 + n_row + n_tile:]:
        o_ref[...] = res.astype(o_ref.dtype)


def _matmul_tiles(M, K, N, w_bytes, out_bytes, n_tiles):
    tm = _pick(M, (2064, 1376, 688, 512, 256, 128, 64, 16))
    if K > 8192:
        tm = _pick(M, (688, 512, 256, 128, 64, 16))
    budget = VMEM_LIMIT - 6 * 1024 * 1024
    for x_bufs in (2, 1):
        for tn in (512, 256, 128):
            if N % tn:
                continue
            need = (x_bufs * tm * K * 2 + 2 * K * tn * w_bytes + (K * tn * 2 if w_bytes > 2 else 0)
                    + 2 * tm * tn * (out_bytes + 4 * n_tiles) + tm * tn * 4)
            if need <= budget:
                return tm, tn, x_bufs
    return tm, _pick(N, (128,)), 1


def _matmul(x, w, *, out_dtype, epilogue=_epi_none, rows=(), tiles=(), widx=(), n=None, col_off=0, name):
    M, K = x.shape
    N = w.shape[-1] if n is None else n
    dtypes = out_dtype if isinstance(out_dtype, tuple) else (out_dtype,)
    tm, tn, x_bufs = _matmul_tiles(M, K, N, w.dtype.itemsize, sum(jnp.dtype(d).itemsize for d in dtypes),
                                   len(tiles))
    assert col_off % tn == 0 and w.ndim == 2 + len(widx)
    joff = col_off // tn
    x_mode = pl.Buffered(1) if x_bufs == 1 else None
    in_specs = [pl.BlockSpec((tm, K), lambda i, j: (i, 0), pipeline_mode=x_mode),
                pl.BlockSpec((None,) * len(widx) + (K, tn), lambda i, j: (*widx, 0, j + joff))]
    in_specs += [pl.BlockSpec((1, tn), lambda i, j: (0, j)) for _ in rows]
    in_specs += [pl.BlockSpec((tm, tn), lambda i, j: (i, j)) for _ in tiles]
    outs = pl.pallas_call(
        functools.partial(_mm_kernel, epilogue=epilogue, n_row=len(rows), n_tile=len(tiles)),
        grid=(M // tm, N // tn),
        in_specs=in_specs,
        out_specs=[pl.BlockSpec((tm, tn), lambda i, j: (i, j)) for _ in dtypes],
        out_shape=[jax.ShapeDtypeStruct((M, N), d) for d in dtypes],
        compiler_params=_cparams(("parallel", "arbitrary")),
        name=name,
    )(x, w, *rows, *tiles)
    return outs if isinstance(out_dtype, tuple) else outs[0]


def _resnorm_kernel(*refs, n_next):
    h_ref, y_ref, gp_ref = refs[0], refs[1], refs[2]
    gn_refs = refs[3:3 + n_next]
    hn_ref = refs[3 + n_next]
    xo_refs = refs[4 + n_next:]
    hn = h_ref[...] + _rms(y_ref[...], gp_ref[...])
    hn_ref[...] = hn
    for gn_ref, xo_ref in zip(gn_refs, xo_refs):
        xo_ref[...] = _rms(hn, gn_ref[...]).astype(xo_ref.dtype)


def _resnorm(h, y, g_post, g_next):
    M, D = h.shape
    tr = _pick(M, (192, 128, 64, 16))
    n = len(g_next)
    row = pl.BlockSpec((tr, D), lambda i: (i, 0))
    vec = pl.BlockSpec((1, D), lambda i: (0, 0))
    outs = pl.pallas_call(
        functools.partial(_resnorm_kernel, n_next=n),
        grid=(M // tr,),
        in_specs=[row, row, vec] + [vec] * n,
        out_specs=[row] + [row] * n,
        out_shape=[jax.ShapeDtypeStruct((M, D), F32)] + [jax.ShapeDtypeStruct((M, D), BF16)] * n,
        compiler_params=_cparams(("parallel",)),
        name="resnorm",
    )(h, y, g_post.reshape(1, D), *[g.reshape(1, D) for g in g_next])
    return outs[0], list(outs[1:])


def _normmix_kernel(h_ref, halo_ref, g_ref, shift_ref, mix_ref, *out_refs, tt):
    t = pl.program_id(1)
    g = g_ref[...]
    xn = _rms(h_ref[0], g)
    halo = _rms(halo_ref[0], g)
    prev_row = jnp.where(t == 0, shift_ref[0], halo[7:8, :])
    row = lax.broadcasted_iota(jnp.int32, (tt, 1), 0)
    x_prev = jnp.where(row == 0, prev_row, pltpu.roll(xn, 1, axis=0))
    xx = x_prev - xn
    for i in range(6):
        out_refs[i][0] = (xn + xx * mix_ref[i:i + 1, :]).astype(BF16)
    out_refs[6][0] = xn[tt - 1:tt, :]


def _normmix(h, g, shift0, mix):
    B, T, D = h.shape
    tt = _pick(T, (64, 48, 32, 16))
    nb = tt // 8
    tok = pl.BlockSpec((1, tt, D), lambda b, t: (b, t, 0))
    outs = pl.pallas_call(
        functools.partial(_normmix_kernel, tt=tt),
        grid=(B, T // tt),
        in_specs=[tok,
                  pl.BlockSpec((1, 8, D), lambda b, t: (b, jnp.maximum(t * nb - 1, 0), 0)),
                  pl.BlockSpec((1, D), lambda b, t: (0, 0)),
                  pl.BlockSpec((1, 1, D), lambda b, t: (b, 0, 0)),
                  pl.BlockSpec((6, D), lambda b, t: (0, 0))],
        out_specs=[tok] * 6 + [pl.BlockSpec((1, 1, D), lambda b, t: (b, 0, 0))],
        out_shape=[jax.ShapeDtypeStruct((B, T, D), BF16)] * 6 + [jax.ShapeDtypeStruct((B, 1, D), F32)],
        compiler_params=_cparams(("parallel", "arbitrary")),
        name="normmix",
    )(h, h, g.reshape(1, D), shift0.reshape(B, 1, D), mix)
    return list(outs[:6]), outs[6].reshape(B, D)


def _wkv_phase1(c0, n_valid, lanes, refs, prm, bones):
    r_ref, k_ref, v_ref, ld_ref, a_ref = refs
    kkw, kaw, rkw = prm
    G, LW, C = WKV_GROUP, WKV_LANES, WKV_CHUNK
    R = G * C
    sl = pl.ds(c0, C)
    r = r_ref[0, sl, lanes]
    k = k_ref[0, sl, lanes]
    v = v_ref[0, sl, lanes]
    ld = ld_ref[0, sl, lanes]
    a = a_ref[0, sl, lanes]
    if n_valid < C:
        valid = lax.broadcasted_iota(jnp.int32, (C, 1), 0) < n_valid
        r, k, v, ld = (jnp.where(valid, x, 0.0) for x in (r, k, v, ld))

    kk = k * kkw
    km = k * (1.0 + (a - 1.0) * kaw)
    ti = lax.broadcasted_iota(jnp.int32, (C, C), 0)
    si = lax.broadcasted_iota(jnp.int32, (C, C), 1)
    kk_ss = _split_dot(kk * kk, bones, 2)
    rk_sum = _split_dot(r * km * rkw, bones, 2)
    cum = _split_dot_lhs(jnp.where(ti >= si, 1.0, 0.0).astype(BF16), ld, 3)
    yield
    kk = kk / jnp.maximum(jnp.sqrt(kk_ss), 1e-12)
    bonus = rk_sum * v
    b = kk * a
    cum_last = cum[C - 1:C, :]
    e_neg = jnp.exp(-cum)
    e_rem = jnp.exp(cum_last - cum)
    lane_head = lax.broadcasted_iota(jnp.int32, (1, LW), 1) // HEAD_A

    def stack(x):
        return jnp.concatenate([jnp.where(lane_head == h, x, 0.0) for h in range(G)], axis=0)

    r_s = stack(jnp.exp(cum) * r)
    a_b = stack(jnp.exp(cum - ld) * (-kk)).astype(BF16)
    v_b = stack(v).astype(BF16)
    ar_b = jnp.concatenate([a_b, r_s.astype(BF16)], axis=0)
    bk_b = jnp.concatenate([stack(e_neg * b), stack(e_neg * km)], axis=0).astype(BF16)
    bkh_t = jnp.concatenate([stack(e_rem * b).T, stack(e_rem * km).T], axis=1).astype(BF16)

    big = _bdot_nt(ar_b, bk_b)
    yield
    row = lax.broadcasted_iota(jnp.int32, (R, R), 0)
    col = lax.broadcasted_iota(jnp.int32, (R, R), 1)
    same = (row // C) == (col // C)
    strict = same & (row > col)
    incl = same & (row >= col)
    n_ab = jnp.where(strict, big[:R, :R], 0.0)
    a_ak = jnp.where(strict, big[:R, R:], 0.0)
    a_rb = jnp.where(incl, big[R:, :R], 0.0).astype(BF16)
    a_rk = jnp.where(incl, big[R:, R:], 0.0)
    av = _bdot(jnp.concatenate([a_ak, a_rk], axis=0), v_b)

    tinv = jnp.where(row == col, 1.0, 0.0) + n_ab
    npow = n_ab
    span = 2
    while span < C:
        npow_b = npow.astype(BF16)
        new_pow = _bdot(npow_b, npow_b)
        if span > 2:
            tinv = tinv + _bdot(tinv, npow_b)
        npow = new_pow
        span *= 2
        yield
    tinv = tinv + _bdot(tinv, npow)
    yield

    wu_b = _bdot(tinv, jnp.concatenate([a_b, av[:R].astype(BF16)], axis=1)).astype(BF16)
    yield
    qo = _bdot(a_rb, wu_b)
    low = jnp.concatenate([jnp.zeros((R, LW), BF16), v_b], axis=1)
    m = _bdot(bkh_t, jnp.concatenate([wu_b, low], axis=0))
    yield
    q_b = (r_s + qo[:, :LW]).astype(BF16)
    o0 = qo[:, LW:] + av[R:]
    li = lax.broadcasted_iota(jnp.int32, (LW, LW), 0)
    lj = lax.broadcasted_iota(jnp.int32, (LW, LW), 1)
    m1_b = (jnp.where(li == lj, jnp.exp(cum_last), 0.0) + m[:, :LW]).astype(BF16)
    return jnp.concatenate([q_b, m1_b], axis=0), o0, m[:, LW:], bonus


def _wkv_phase2(state, ph, c0, n_valid, lanes, g_ref, y_ref, lnw, lnb, bones):
    qm_b, o0, m2, bonus = ph
    C = WKV_CHUNK
    R = WKV_GROUP * C
    qs = _bdot(qm_b, state)
    yield
    o_st = qs[:R] + o0
    new_state = qs[R:] + m2
    o = o_st[0:C, :]
    for h in range(1, WKV_GROUP):
        o = o + o_st[h * C:(h + 1) * C, :]
    mu = _split_dot(o, bones, 2) * (1.0 / HEAD_A)
    yield
    d = o - mu
    var = _split_dot(d * d, bones, 2) * (1.0 / HEAD_A)
    yield
    o = d * lax.rsqrt(var + GN_EPS) * lnw + lnb + bonus
    sl = pl.ds(c0, n_valid)
    y_ref[0, sl, lanes] = (o[:n_valid] * g_ref[0, sl, lanes]).astype(y_ref.dtype)
    return new_state


def _wkv_kernel(r_ref, k_ref, v_ref, ld_ref, a_ref, g_ref, kkw_ref, kaw_ref, rkw_ref, lnw_ref, lnb_ref,
                s0_ref, y_ref, sout_ref, *, T, n_tiles, n_par):
    G, LW, N = WKV_GROUP, WKV_LANES, HEAD_A
    li = lax.broadcasted_iota(jnp.int32, (LW, LW), 0)
    lj = lax.broadcasted_iota(jnp.int32, (LW, LW), 1)
    bones = jnp.where((li // N) == (lj // N), 1.0, 0.0).astype(BF16)
    refs = (r_ref, k_ref, v_ref, ld_ref, a_ref)
    lane_sl = [slice(p * LW, (p + 1) * LW) for p in range(n_tiles)]
    prm = [(kkw_ref[:, s], kaw_ref[:, s], rkw_ref[:, s]) for s in lane_sl]
    post = [(lnw_ref[:, s], lnb_ref[:, s]) for s in lane_sl]

    def steps(c_base, n_valid, n_chunks, states):
        C = WKV_CHUNK
        ph = _interleave([_wkv_phase1(c_base + u * C, n_valid, lane_sl[p], refs, prm[p], bones)
                          for u in range(n_chunks) for p in range(n_tiles)])
        states = list(states)
        for u in range(n_chunks):
            states = _interleave([
                _wkv_phase2(states[p].astype(BF16), ph[u * n_tiles + p], c_base + u * C, n_valid, lane_sl[p],
                            g_ref, y_ref, post[p][0], post[p][1], bones) for p in range(n_tiles)])
        return tuple(states)

    zero = jnp.zeros((N, N), F32)
    states = []
    for p in range(n_tiles):
        blocks = [jnp.concatenate([s0_ref[0, p * G + h].T if j == h else zero for j in range(G)], axis=1)
                  for h in range(G)]
        states.append(jnp.concatenate(blocks, axis=0))
    states = tuple(states)

    head = T % WKV_CHUNK
    if head:
        states = steps(0, head, 1, states)
    n_main = T // WKV_CHUNK
    if n_main:
        def body(i, st):
            return steps(pl.multiple_of(head + i * (n_par * WKV_CHUNK), 16), WKV_CHUNK, n_par, st)

        states = lax.fori_loop(0, n_main // n_par, body, states)

    for p in range(n_tiles):
        for h in range(G):
            sout_ref[0, p * G + h] = states[p][h * N:(h + 1) * N, h * N:(h + 1) * N].T


def _wkv(r, k, v, ld, a, g, kkw, kaw, rkw, lnw, lnb, s0):
    B, T, D = r.shape
    LW, G = WKV_LANES, WKV_GROUP
    n_main = T // WKV_CHUNK
    assert T >= WKV_CHUNK
    n_par = _pick(n_main, (4, 2, 1))
    n_tiles = _pick(D // LW, (2,)) if n_main > 1 else _pick(D // LW, (4, 2))
    bw = n_tiles * LW
    tok = pl.BlockSpec((1, T, bw), lambda b, j: (b, 0, j))
    vec = pl.BlockSpec((1, bw), lambda b, j: (0, j))
    st = pl.BlockSpec((1, n_tiles * G, HEAD_A, HEAD_A), lambda b, j: (b, j, 0, 0))
    y, s_out = pl.pallas_call(
        functools.partial(_wkv_kernel, T=T, n_tiles=n_tiles, n_par=n_par),
        grid=(B, D // bw),
        in_specs=[tok] * 6 + [vec] * 5 + [st],
        out_specs=[tok, st],
        out_shape=[jax.ShapeDtypeStruct((B, T, D), BF16), jax.ShapeDtypeStruct(s0.shape, F32)],
        compiler_params=_cparams(("parallel", "parallel")),
        name="wkv7",
    )(r, k, v, ld, a, g, *[p.reshape(1, D) for p in (kkw, kaw, rkw, lnw, lnb)], s0)
    return y, s_out


def _sb_consts(nk):
    ji = lax.broadcasted_iota(jnp.int32, (nk, nk), 0)
    si = lax.broadcasted_iota(jnp.int32, (nk, nk), 1)
    later = jnp.where(ji > si, 1.0, 0.0).astype(BF16)
    return jnp.concatenate([later, later], axis=0)


def _sb_pair(q, kt, vt, carry, acc, later2, read):
    nq, nk = q.shape[0], kt.shape[0]
    z = _bdot_nt(q, kt) * (HEAD_B ** -0.5)
    yield
    lk = -(jnp.maximum(z, 0.0) + jnp.log(1.0 + jnp.exp(-jnp.abs(z))))
    lkm = lk if read is None else jnp.where(read, lk, 0.0)
    hi, lo = _split(lkm, 2)
    if nk % V7X_LANES == 0:
        local = jnp.dot(jnp.concatenate([hi, lo], axis=1), later2, preferred_element_type=F32)
    else:
        local = (jnp.dot(hi, later2[:nk], preferred_element_type=F32)
                 + jnp.dot(lo, later2[:nk], preferred_element_type=F32))
    yield
    total = jnp.broadcast_to(local[:, 0:1] + lkm[:, 0:1], (nq, V7X_LANES))
    if nk % V7X_LANES == 0:
        after = local + jnp.concatenate([carry] * (nk // V7X_LANES), axis=1)
    else:
        after = local + carry[:, :nk]
    w = jnp.exp(z + lk + after)
    if read is not None:
        w = jnp.where(read, w, 0.0)
    return carry + total, acc + _bdot(w, vt)


def _attn_kernel(*refs, T, P, KN, KP):
    if P:
        q_ref, kn_ref, vn_ref, kp_ref, vp_ref, o_ref = refs
    else:
        q_ref, kn_ref, vn_ref, o_ref = refs
    TQ, HB = ATT_TQ, HEAD_B
    n_heads = q_ref.shape[2] // HB
    lanes = [slice(h * HB, (h + 1) * HB) for h in range(n_heads)]
    n_full, rem = T // TQ, T % TQ
    c_new = _sb_consts(KN)
    c_past = _sb_consts(KP) if P else None

    def sweep(qs, ca, n_new):
        def step(k_ref, v_ref, width, consts, jj, ca):
            rows = pl.ds(pl.multiple_of(jj * width, width), width)
            return tuple(_interleave([_sb_pair(qs[h], k_ref[0, rows, lanes[h]], v_ref[0, rows, lanes[h]],
                                               ca[h][0], ca[h][1], consts, None) for h in range(n_heads)]))

        ca = lax.fori_loop(0, n_new, lambda j, c: step(kn_ref, vn_ref, KN, c_new, n_new - 1 - j, c), ca)
        if P:
            n_past = P // KP
            ca = lax.fori_loop(0, n_past, lambda j, c: step(kp_ref, vp_ref, KP, c_past, n_past - 1 - j, c), ca)
        return ca

    def query_tile(q0, nq, k0, nk, off, consts):
        qrows = pl.ds(q0, nq)
        krows = pl.ds(k0, nk)
        qs = [q_ref[0, qrows, lanes[h]] for h in range(n_heads)]
        cmr = lax.broadcasted_iota(jnp.int32, (nq, nk), 1) - lax.broadcasted_iota(jnp.int32, (nq, nk), 0)
        read = cmr < off
        zc = jnp.zeros((nq, V7X_LANES), F32)
        za = jnp.zeros((nq, HB), F32)
        ca = tuple(_interleave([_sb_pair(qs[h], kn_ref[0, krows, lanes[h]], vn_ref[0, krows, lanes[h]],
                                         zc, za, consts, read) for h in range(n_heads)]))
        ca = sweep(qs, ca, k0 // KN)
        for h in range(n_heads):
            o_ref[0, qrows, lanes[h]] = ca[h][1].astype(o_ref.dtype)

    if n_full:
        def full_tile(i, c):
            q0 = pl.multiple_of(i * TQ, TQ)
            k0 = pl.multiple_of((q0 // KN) * KN, KN)
            query_tile(q0, TQ, k0, KN, q0 - k0, c_new)
            return c

        lax.fori_loop(0, n_full, full_tile, 0)
    if rem:
        q0 = n_full * TQ
        query_tile(q0, rem, q0, rem, 0, _sb_consts(rem))


def _attention(q, k_new, v_new, k_past, v_past):
    B, T, D = q.shape
    P = 0 if k_past is None else k_past.shape[1]
    full = T - T % ATT_TQ
    KN = V7X_MXU_DIM if full % V7X_MXU_DIM == 0 and full else ATT_TQ
    KP = V7X_MXU_DIM if P % V7X_MXU_DIM == 0 else ATT_TQ
    assert P % KP == 0 and full % KN == 0
    bw = HEAD_B * _pick(D // HEAD_B, (ATT_HEADS, 2, 1))
    new = pl.BlockSpec((1, T, bw), lambda b, h: (b, 0, h))
    in_specs = [new, new, new]
    args = [q, k_new, v_new]
    if P:
        past = pl.BlockSpec((1, P, bw), lambda b, h: (b, 0, h))
        in_specs += [past, past]
        args += [k_past, v_past]
    return pl.pallas_call(
        functools.partial(_attn_kernel, T=T, P=P, KN=KN, KP=KP),
        grid=(B, D // bw),
        in_specs=in_specs,
        out_specs=new,
        out_shape=jax.ShapeDtypeStruct((B, T, D), BF16),
        compiler_params=_cparams(("parallel", "parallel")),
        name="sb_attention",
    )(*args)


def _ffn_up_kernel(x_ref, wg_ref, wu_ref, st_ref, cw_ref, cb_ref, act_ref, nst_ref, *, T, spt, n_sub):
    tm, tn = act_ref.shape
    rs = tm // n_sub
    wg = wg_ref[...].astype(BF16)
    wu = wu_ref[...].astype(BF16)
    cw = cw_ref[...]
    cb = cb_ref[...]
    st = st_ref[...]
    if spt == 1:
        s0, s1 = st[0, 0:1, :], st[0, 1:2, :]
    else:
        s0 = jnp.broadcast_to(st[:, 0:1, :], (spt, T, tn)).reshape(tm, tn)
        s1 = jnp.broadcast_to(st[:, 1:2, :], (spt, T, tn)).reshape(tm, tn)
    lrow = lax.broadcasted_iota(jnp.int32, (rs, 1), 0)
    prev = None
    for c in range(n_sub):
        x = x_ref[c * rs:(c + 1) * rs, :]
        gate = jnp.dot(x, wg, preferred_element_type=F32)
        up = jnp.dot(x, wu, preferred_element_type=F32)
        g1 = pltpu.roll(gate, 1, axis=0)
        g2 = pltpu.roll(gate, 2, axis=0)
        if prev is not None:
            g1 = jnp.where(lrow == 0, prev[1:2, :], g1)
            g2 = jnp.where(lrow == 0, prev[0:1, :], jnp.where(lrow == 1, prev[1:2, :], g2))
        pos = (lrow + c * rs) % T
        g1 = jnp.where(pos == 0, s1, g1)
        g2 = jnp.where(pos == 0, s0, jnp.where(pos == 1, s1, g2))
        conv = g2 * cw[0:1, :] + g1 * cw[1:2, :] + gate * cw[2:3, :] + cb
        act_ref[c * rs:(c + 1) * rs, :] = (jax.nn.silu(conv) * up).astype(act_ref.dtype)
        prev = gate[rs - 2:rs, :]
        if spt > 1:
            nst_ref[...] = gate.reshape(spt, T, tn)[:, T - 2:T, :]
    if spt == 1:
        nst_ref[0] = prev


def _ffn_up(x, w_up, layer, conv_state, conv_w, conv_b, T):
    M, D = x.shape
    F = w_up.shape[-1] // 2
    n_seq = M // T
    spt = _pick(n_seq, (8, 4, 2, 1)) if T * 8 <= 512 else 1
    tm = spt * T
    n_sub = 1
    if spt == 1:
        n_sub = next((c for c in (3, 2, 4) if tm % (16 * c) == 0 and tm // c >= 128), 1)
    tn = _pick(F, (256, 128))
    nf = F // tn
    x_mode = pl.Buffered(1) if tm * D * 2 > 8 * 1024 * 1024 else None
    st_spec = pl.BlockSpec((spt, 2, tn), lambda i, j: (i, 0, j))
    return pl.pallas_call(
        functools.partial(_ffn_up_kernel, T=T, spt=spt, n_sub=n_sub),
        grid=(M // tm, nf),
        in_specs=[pl.BlockSpec((tm, D), lambda i, j: (i, 0), pipeline_mode=x_mode),
                  pl.BlockSpec((None, D, tn), lambda i, j: (layer, 0, j)),
                  pl.BlockSpec((None, D, tn), lambda i, j: (layer, 0, j + nf)),
                  st_spec,
                  pl.BlockSpec((CONV_W, tn), lambda i, j: (0, j)),
                  pl.BlockSpec((1, tn), lambda i, j: (0, j))],
        out_specs=[pl.BlockSpec((tm, tn), lambda i, j: (i, j)), st_spec],
        out_shape=[jax.ShapeDtypeStruct((M, F), BF16), jax.ShapeDtypeStruct((n_seq, 2, F), F32)],
        compiler_params=_cparams(("parallel", "arbitrary")),
        name="ffn_up_conv",
    )(x, w_up, w_up, conv_state, conv_w, conv_b.reshape(1, F))


def _run_group(h, shift0, wkv0, conv0, past_k, past_v, p, depth, n_a):
    B, T, D = h.shape
    M = B * T
    flat = lambda t: t.reshape(M, D)
    seq = lambda t: t.reshape(B, T, D)
    gains = p['norm_gains']
    new_wkv, new_shift, new_conv = [], [], []
    v_first = None
    k_new = v_new = k_bf = v_bf = None
    h = flat(h)
    xin = None
    xkv = None
    for layer in range(depth):
        g = gains[layer]
        if layer < n_a:
            i = layer
            mixed_in, last_row = _normmix(seq(h), g[0], shift0[i], p['a_mix'][i])
            xr, xw, xk, xv, xa, xg = (flat(t) for t in mixed_in)
            new_shift.append(last_row)
            r = _matmul(xr, p['a_w_rkv'], widx=(i, 0), out_dtype=F32, name="rwkv_r")
            k = _matmul(xk, p['a_w_rkv'], widx=(i, 1), out_dtype=F32, name="rwkv_k")
            v = _matmul(xv, p['a_w_rkv'], widx=(i, 2), out_dtype=F32, name="rwkv_v")
            ld = _matmul(_matmul(xw, p['a_w1'], widx=(i,), out_dtype=BF16, epilogue=_epi_tanh, name="lora_w1"),
                         p['a_w2'], widx=(i,), out_dtype=F32, epilogue=_epi_log_decay,
                         rows=(p['a_w0'][i].reshape(1, D),), name="lora_w2")
            a = _matmul(_matmul(xa, p['a_a1'], widx=(i,), out_dtype=BF16, name="lora_a1"),
                        p['a_a2'], widx=(i,), out_dtype=F32, epilogue=_epi_bias_sigmoid,
                        rows=(p['a_a0'][i].reshape(1, D),), name="lora_a2")
            gate = _matmul(_matmul(xg, p['a_g1'], widx=(i,), out_dtype=BF16, epilogue=_epi_sigmoid,
                                   name="lora_g1"),
                           p['a_g2'], widx=(i,), out_dtype=F32, name="lora_g2")
            if i == 0:
                v_first = v
            else:
                v = _matmul(_matmul(xv, p['a_v1'], widx=(i - 1,), out_dtype=BF16, name="lora_v1"),
                            p['a_v2'], widx=(i - 1,), out_dtype=F32, epilogue=_epi_value_mix,
                            rows=(p['a_v0'][i - 1].reshape(1, D),), tiles=(v, v_first), name="lora_v2")
            y, s_out = _wkv(seq(r), seq(k), seq(v), seq(ld), seq(a), seq(gate),
                            p['a_k_k'][i], p['a_k_a'][i], p['a_r_k'][i].reshape(D), p['a_ln_w'][i],
                            p['a_ln_b'][i], wkv0[i])
            new_wkv.append(s_out)
            mixed = _matmul(flat(y), p['a_w_out'], widx=(i,), out_dtype=F32, name="rwkv_out")
        else:
            j = layer - n_a
            if layer == n_a:
                k_new, k_bf = _matmul(xkv, p['w_kv'], out_dtype=(F32, BF16), n=D, name="kv_proj_k")
                v_new, v_bf = _matmul(xkv, p['w_kv'], out_dtype=(F32, BF16), n=D, col_off=D, name="kv_proj_v")
            q = _matmul(xin, p['b_w_q'], widx=(j,), out_dtype=BF16, name="attn_q")
            attn = _attention(seq(q), seq(k_bf), seq(v_bf), past_k, past_v)
            mixed = _matmul(flat(attn), p['b_w_out'], widx=(j,), out_dtype=F32, name="attn_out")
        h, (xf,) = _resnorm(h, mixed, g[1], [g[2]])
        act, conv_new = _ffn_up(xf, p['ffn_w_up'], layer, conv0[layer], p['ffn_conv_w'][layer],
                                p['ffn_conv_b'][layer], T)
        new_conv.append(conv_new)
        f = _matmul(act, p['ffn_w_down'], widx=(layer,), out_dtype=F32, name="ffn_down")
        nxt = []
        if layer + 1 < depth and layer + 1 >= n_a:
            nxt.append(gains[layer + 1][0])
            if layer + 1 == n_a:
                nxt.append(p['kv_norm'])
        h, normed = _resnorm(h, f, g[3], nxt)
        if normed:
            xin = normed[0]
            if len(normed) > 1:
                xkv = normed[1]
    return (seq(h), seq(k_new), seq(v_new), jnp.stack(new_wkv), jnp.stack(new_shift),
            jnp.stack(new_conv))


def kernel(x_prompt, x_sample, cache_k, cache_v, state_wkv, state_shift, state_ffn_conv, meta_tokens, norm_gains, a_mix, a_w_rkv, a_w_out, a_w0, a_w1, a_w2, a_a0, a_a1, a_a2, a_v0, a_v1, a_v2, a_g1, a_g2, a_k_k, a_k_a, a_r_k, a_ln_w, a_ln_b, kv_norm, w_kv, b_w_q, b_w_out, ffn_w_up, ffn_conv_w, ffn_conv_b, ffn_w_down):
    B, S, D = x_prompt.shape
    Bs, Ts, _ = x_sample.shape
    depth = norm_gains.shape[0]
    n_a = a_mix.shape[0]
    F = ffn_w_down.shape[1]
    H_a, H_b = D // HEAD_A, D // HEAD_B
    assert n_a >= 1 and depth > n_a
    bf = lambda w: w.astype(BF16)
    p = {
        'norm_gains': norm_gains, 'a_mix': a_mix, 'a_w_rkv': a_w_rkv, 'a_w_out': a_w_out,
        'a_w0': a_w0, 'a_w1': a_w1, 'a_w2': a_w2, 'a_a0': a_a0, 'a_a1': a_a1,
        'a_a2': a_a2, 'a_v0': a_v0, 'a_v1': a_v1, 'a_v2': a_v2, 'a_g1': a_g1,
        'a_g2': a_g2, 'a_k_k': a_k_k, 'a_k_a': a_k_a, 'a_r_k': a_r_k, 'a_ln_w': a_ln_w,
        'a_ln_b': a_ln_b, 'kv_norm': kv_norm, 'w_kv': w_kv, 'b_w_q': b_w_q,
        'b_w_out': b_w_out, 'ffn_w_up': ffn_w_up, 'ffn_conv_w': ffn_conv_w,
        'ffn_conv_b': ffn_conv_b, 'ffn_w_down': bf(ffn_w_down),
    }
    dt = x_prompt.dtype
    T = N_META + S
    meta = jnp.broadcast_to(meta_tokens[None].astype(dt), (B, N_META, D))
    h0 = jnp.concatenate([meta, x_prompt], axis=1)
    h_p, k_p, v_p, wkv_p, shift_p, conv_p = _run_group(
        h0, jnp.zeros((n_a, B, D), dt), jnp.zeros((n_a, B, H_a, HEAD_A, HEAD_A), dt),
        jnp.zeros((depth, B, CONV_W - 1, F), dt), None, None, p, depth, n_a)
    P = cache_k.shape[1]
    h_s, k_s, v_s, wkv_s, shift_s, conv_s = _run_group(
        x_sample, state_shift, state_wkv, state_ffn_conv, bf(cache_k).reshape(Bs, P, D),
        bf(cache_v).reshape(Bs, P, D), p, depth, n_a)
    heads = lambda t: t.reshape(t.shape[0], t.shape[1], H_b, HEAD_B)
    return (h_p[:, N_META:], h_s, heads(k_p), heads(v_p), heads(k_s), heads(v_s),
            wkv_p, wkv_s, shift_p, shift_s, conv_p, conv_s)
```

```python
import functools

import jax
import jax.numpy as jnp
from jax import lax
from jax.experimental import pallas as pl
from jax.experimental.pallas import tpu as pltpu

F32 = jnp.float32
BF16 = jnp.bfloat16

N_META = 16
HEAD_A = 64
HEAD_B = 128
GN_EPS = 64e-5
RMS_EPS = 1e-6
CONV_W = 3

V7X_LANES = 128
V7X_MXU_DIM = 256
V7X_VMEM_BYTES = 64 * 1024 * 1024
VMEM_LIMIT = V7X_VMEM_BYTES - 8 * 1024 * 1024

W_CAST_CHUNK = 1024
KGRID_W_BYTES = 8 * 1024 * 1024
LOG2E = 1.4426950408889634

WKV_GROUP = 2
WKV_LANES = WKV_GROUP * HEAD_A
WKV_CHUNK = 64
ATT_TQ = 128
ATT_HEADS = 8


def _pick(n, prefs):
    for p in prefs:
        if p <= n and n % p == 0:
            return p
    return n


def _cparams(sem):
    return pltpu.CompilerParams(dimension_semantics=sem, vmem_limit_bytes=VMEM_LIMIT)


def _bdot(a, b):
    return jnp.dot(a.astype(BF16), b.astype(BF16), preferred_element_type=F32)


def _bdot_nt(a, b):
    return lax.dot_general(a.astype(BF16), b.astype(BF16), (((1,), (1,)), ((), ())),
                           preferred_element_type=F32)


def _bdot_tn(a, b):
    return lax.dot_general(a.astype(BF16), b.astype(BF16), (((0,), (0,)), ((), ())),
                           preferred_element_type=F32)


def _split(x, parts):
    out = []
    rem = x
    for i in range(parts):
        hi = rem.astype(BF16)
        out.append(hi)
        if i + 1 < parts:
            rem = rem - hi.astype(F32)
    return out


def _split_dot(x, m_bf16, parts):
    acc = None
    for hi in _split(x, parts):
        t = jnp.dot(hi, m_bf16, preferred_element_type=F32)
        acc = t if acc is None else acc + t
    return acc


def _split_dot_lhs(m_bf16, x, parts):
    acc = None
    for hi in _split(x, parts):
        t = jnp.dot(m_bf16, hi, preferred_element_type=F32)
        acc = t if acc is None else acc + t
    return acc


def _interleave(gens):
    results = [None] * len(gens)
    live = list(range(len(gens)))
    while live:
        still = []
        for i in live:
            try:
                next(gens[i])
                still.append(i)
            except StopIteration as stop:
                results[i] = stop.value
        live = still
    return results


def _rms(x, g):
    return x * lax.rsqrt(jnp.mean(x * x, axis=-1, keepdims=True) + RMS_EPS) * g


def _epi_none(acc):
    return acc


def _epi_tanh(acc):
    return jnp.tanh(acc)


def _epi_sigmoid(acc):
    return jax.nn.sigmoid(acc)


def _epi_bias_sigmoid(acc, bias):
    return jax.nn.sigmoid(bias + acc)


def _epi_log_decay(acc, w0):
    return -jnp.exp(-jax.nn.softplus(-(w0 + acc)) - 0.5)


def _epi_value_mix(acc, v0, v, v_first):
    return v + (v_first - v) * jax.nn.sigmoid(v0 + acc)


def _w_chunks(w_ref):
    K = w_ref.shape[0]
    if w_ref.dtype == BF16 or K % W_CAST_CHUNK or K == W_CAST_CHUNK:
        return [(slice(None), w_ref[...].astype(BF16))]
    return [(slice(k0, k0 + W_CAST_CHUNK), w_ref[k0:k0 + W_CAST_CHUNK, :].astype(BF16))
            for k0 in range(0, K, W_CAST_CHUNK)]


def _dot_chunks(x_ref, rows, chunks):
    acc = None
    for ks, w in chunks:
        t = jnp.dot(x_ref[rows, ks], w, preferred_element_type=F32)
        acc = t if acc is None else acc + t
    return acc


def _dot_cast_w(x_ref, w_ref):
    return _dot_chunks(x_ref, slice(None), _w_chunks(w_ref))


def _mm_kernel(*refs, epilogue, n_row, n_tile):
    x_ref, w_ref = refs[0], refs[1]
    rows = [r[...] for r in refs[2:2 + n_row]]
    tiles = [r[...] for r in refs[2 + n_row:2 + n_row + n_tile]]
    acc = _dot_cast_w(x_ref, w_ref)
    res = epilogue(acc, *rows, *tiles)
    for o_ref in refs[2 + n_row + n_tile:]:
        o_ref[...] = res.astype(o_ref.dtype)


def _matmul_tiles(M, K, N, w_bytes, out_bytes, n_tiles):
    tm = _pick(M, (2064, 1376, 688, 512, 256, 128, 64, 16))
    if K > 8192:
        tm = _pick(M, (688, 512, 256, 128, 64, 16))
    budget = VMEM_LIMIT - 6 * 1024 * 1024
    for tn in (512, 256, 128):
        for x_bufs in (2, 1):
            if N % tn:
                continue
            need = (x_bufs * tm * K * 2 + 2 * K * tn * w_bytes + (W_CAST_CHUNK * tn * 2 if w_bytes > 2 else 0)
                    + 2 * tm * tn * (out_bytes + 4 * n_tiles) + tm * tn * 4)
            if need <= budget:
                return tm, tn, x_bufs
    return tm, _pick(N, (128,)), 1


def _mmk_kernel(*refs, epilogue, n_row, n_tile, n_out):
    x_ref, w_ref = refs[0], refs[1]
    row_refs = refs[2:2 + n_row]
    tile_refs = refs[2 + n_row:2 + n_row + n_tile]
    out_refs = refs[2 + n_row + n_tile:2 + n_row + n_tile + n_out]
    acc_ref = refs[-1]
    k = pl.program_id(0)
    part = jnp.dot(x_ref[...], w_ref[...].astype(BF16), preferred_element_type=F32)

    @pl.when(k == 0)
    def _():
        acc_ref[...] = part

    @pl.when(k > 0)
    def _():
        acc_ref[...] += part

    @pl.when(k == pl.num_programs(0) - 1)
    def _():
        res = epilogue(acc_ref[...], *[r[...] for r in row_refs], *[t[...] for t in tile_refs])
        for o_ref in out_refs:
            o_ref[...] = res.astype(o_ref.dtype)


def _matmul_kgrid(x, w, dtypes, epilogue, rows, tiles, widx, N, col_off, name, as_tuple):
    M, K = x.shape
    tk = _pick(K, tuple(t for t in (2048, 1024, 512, 256, 128) if t * N * w.dtype.itemsize <= KGRID_W_BYTES))
    jblk = col_off // N
    full = pl.BlockSpec((M, N), lambda k: (0, 0))
    in_specs = [pl.BlockSpec((M, tk), lambda k: (0, k)),
                pl.BlockSpec((None,) * len(widx) + (tk, N), lambda k: (*widx, k, jblk))]
    in_specs += [pl.BlockSpec((1, N), lambda k: (0, 0)) for _ in rows] + [full for _ in tiles]
    outs = pl.pallas_call(
        functools.partial(_mmk_kernel, epilogue=epilogue, n_row=len(rows), n_tile=len(tiles), n_out=len(dtypes)),
        grid=(K // tk,),
        in_specs=in_specs,
        out_specs=[full for _ in dtypes],
        out_shape=[jax.ShapeDtypeStruct((M, N), d) for d in dtypes],
        scratch_shapes=[pltpu.VMEM((M, N), F32)],
        compiler_params=_cparams(("arbitrary",)),
        name=name,
    )(x, w, *rows, *tiles)
    return outs if as_tuple else outs[0]


def _matmul(x, w, *, out_dtype, epilogue=_epi_none, rows=(), tiles=(), widx=(), n=None, col_off=0, name):
    M, K = x.shape
    N = w.shape[-1] if n is None else n
    dtypes = out_dtype if isinstance(out_dtype, tuple) else (out_dtype,)
    if M <= 512 and K * N * w.dtype.itemsize >= 2 * KGRID_W_BYTES and col_off % N == 0:
        return _matmul_kgrid(x, w, dtypes, epilogue, rows, tiles, widx, N, col_off, name,
                             isinstance(out_dtype, tuple))
    tm, tn, x_bufs = _matmul_tiles(M, K, N, w.dtype.itemsize, sum(jnp.dtype(d).itemsize for d in dtypes),
                                   len(tiles))
    assert col_off % tn == 0 and w.ndim == 2 + len(widx)
    joff = col_off // tn
    x_mode = pl.Buffered(1) if x_bufs == 1 else None
    in_specs = [pl.BlockSpec((tm, K), lambda i, j: (i, 0), pipeline_mode=x_mode),
                pl.BlockSpec((None,) * len(widx) + (K, tn), lambda i, j: (*widx, 0, j + joff))]
    in_specs += [pl.BlockSpec((1, tn), lambda i, j: (0, j)) for _ in rows]
    in_specs += [pl.BlockSpec((tm, tn), lambda i, j: (i, j)) for _ in tiles]
    outs = pl.pallas_call(
        functools.partial(_mm_kernel, epilogue=epilogue, n_row=len(rows), n_tile=len(tiles)),
        grid=(M // tm, N // tn),
        in_specs=in_specs,
        out_specs=[pl.BlockSpec((tm, tn), lambda i, j: (i, j)) for _ in dtypes],
        out_shape=[jax.ShapeDtypeStruct((M, N), d) for d in dtypes],
        compiler_params=_cparams(("parallel", "arbitrary")),
        name=name,
    )(x, w, *rows, *tiles)
    return outs if isinstance(out_dtype, tuple) else outs[0]


def _resnorm_kernel(*refs, n_next):
    h_ref, y_ref, gp_ref = refs[0], refs[1], refs[2]
    gn_refs = refs[3:3 + n_next]
    hn_ref = refs[3 + n_next]
    xo_refs = refs[4 + n_next:]
    hn = h_ref[...] + _rms(y_ref[...], gp_ref[...])
    hn_ref[...] = hn
    for gn_ref, xo_ref in zip(gn_refs, xo_refs):
        xo_ref[...] = _rms(hn, gn_ref[...]).astype(xo_ref.dtype)


def _resnorm(h, y, g_post, g_next):
    M, D = h.shape
    tr = _pick(M, (192, 128, 64, 16))
    n = len(g_next)
    row = pl.BlockSpec((tr, D), lambda i: (i, 0))
    vec = pl.BlockSpec((1, D), lambda i: (0, 0))
    outs = pl.pallas_call(
        functools.partial(_resnorm_kernel, n_next=n),
        grid=(M // tr,),
        in_specs=[row, row, vec] + [vec] * n,
        out_specs=[row] + [row] * n,
        out_shape=[jax.ShapeDtypeStruct((M, D), F32)] + [jax.ShapeDtypeStruct((M, D), BF16)] * n,
        compiler_params=_cparams(("parallel",)),
        name="resnorm",
    )(h, y, g_post.reshape(1, D), *[g.reshape(1, D) for g in g_next])
    return outs[0], list(outs[1:])


def _normmix_kernel(h_ref, halo_ref, g_ref, shift_ref, mix_ref, *out_refs, tt):
    t = pl.program_id(1)
    g = g_ref[...]
    xn = _rms(h_ref[0], g)
    halo = _rms(halo_ref[0], g)
    prev_row = jnp.where(t == 0, shift_ref[0], halo[7:8, :])
    row = lax.broadcasted_iota(jnp.int32, (tt, 1), 0)
    x_prev = jnp.where(row == 0, prev_row, pltpu.roll(xn, 1, axis=0))
    xx = x_prev - xn
    for i in range(6):
        out_refs[i][0] = (xn + xx * mix_ref[i:i + 1, :]).astype(BF16)
    out_refs[6][0] = xn[tt - 1:tt, :]


def _normmix(h, g, shift0, mix):
    B, T, D = h.shape
    tt = _pick(T, (64, 48, 32, 16))
    nb = tt // 8
    tok = pl.BlockSpec((1, tt, D), lambda b, t: (b, t, 0))
    outs = pl.pallas_call(
        functools.partial(_normmix_kernel, tt=tt),
        grid=(B, T // tt),
        in_specs=[tok,
                  pl.BlockSpec((1, 8, D), lambda b, t: (b, jnp.maximum(t * nb - 1, 0), 0)),
                  pl.BlockSpec((1, D), lambda b, t: (0, 0)),
                  pl.BlockSpec((1, 1, D), lambda b, t: (b, 0, 0)),
                  pl.BlockSpec((6, D), lambda b, t: (0, 0))],
        out_specs=[tok] * 6 + [pl.BlockSpec((1, 1, D), lambda b, t: (b, 0, 0))],
        out_shape=[jax.ShapeDtypeStruct((B, T, D), BF16)] * 6 + [jax.ShapeDtypeStruct((B, 1, D), F32)],
        compiler_params=_cparams(("parallel", "arbitrary")),
        name="normmix",
    )(h, h, g.reshape(1, D), shift0.reshape(B, 1, D), mix)
    return list(outs[:6]), outs[6].reshape(B, D)


def _wkv_phase1(c0, n_valid, lanes, refs, prm, bones):
    r_ref, k_ref, v_ref, ld_ref, a_ref = refs
    kkw, kaw, rkw = prm
    G, LW, C = WKV_GROUP, WKV_LANES, WKV_CHUNK
    R = G * C
    sl = pl.ds(c0, C)
    r = r_ref[0, sl, lanes]
    k = k_ref[0, sl, lanes]
    v = v_ref[0, sl, lanes]
    ld = ld_ref[0, sl, lanes]
    a = a_ref[0, sl, lanes]
    if n_valid < C:
        valid = lax.broadcasted_iota(jnp.int32, (C, 1), 0) < n_valid
        r, k, v, ld = (jnp.where(valid, x, 0.0) for x in (r, k, v, ld))

    kk = k * kkw
    km = k * (1.0 + (a - 1.0) * kaw)
    ti = lax.broadcasted_iota(jnp.int32, (C, C), 0)
    si = lax.broadcasted_iota(jnp.int32, (C, C), 1)
    kk_ss = _split_dot(kk * kk, bones, 2)
    rk_sum = _split_dot(r * km * rkw, bones, 2)
    cum = _split_dot_lhs(jnp.where(ti >= si, 1.0, 0.0).astype(BF16), ld, 3)
    yield
    kk = kk / jnp.maximum(jnp.sqrt(kk_ss), 1e-12)
    bonus = rk_sum * v
    b = kk * a
    cum_last = cum[C - 1:C, :]
    e_neg = jnp.exp(-cum)
    e_rem = jnp.exp(cum_last - cum)
    lane_head = lax.broadcasted_iota(jnp.int32, (1, LW), 1) // HEAD_A

    def stack(x):
        return jnp.concatenate([jnp.where(lane_head == h, x, 0.0) for h in range(G)], axis=0)

    r_s = stack(jnp.exp(cum) * r)
    a_b = stack(jnp.exp(cum - ld) * (-kk)).astype(BF16)
    v_b = stack(v).astype(BF16)
    ar_b = jnp.concatenate([a_b, r_s.astype(BF16)], axis=0)
    bk_b = jnp.concatenate([stack(e_neg * b), stack(e_neg * km)], axis=0).astype(BF16)
    bkh_t = jnp.concatenate([stack(e_rem * b).T, stack(e_rem * km).T], axis=1).astype(BF16)

    big = _bdot_nt(ar_b, bk_b)
    yield
    row = lax.broadcasted_iota(jnp.int32, (R, R), 0)
    col = lax.broadcasted_iota(jnp.int32, (R, R), 1)
    same = (row // C) == (col // C)
    strict = same & (row > col)
    incl = same & (row >= col)
    n_ab = jnp.where(strict, big[:R, :R], 0.0)
    a_ak = jnp.where(strict, big[:R, R:], 0.0)
    a_rb = jnp.where(incl, big[R:, :R], 0.0).astype(BF16)
    a_rk = jnp.where(incl, big[R:, R:], 0.0)
    av = _bdot(jnp.concatenate([a_ak, a_rk], axis=0), v_b)

    tinv = jnp.where(row == col, 1.0, 0.0) + n_ab
    npow = n_ab
    span = 2
    while span < C:
        npow_b = npow.astype(BF16)
        new_pow = _bdot(npow_b, npow_b)
        if span > 2:
            tinv = tinv + _bdot(tinv, npow_b)
        npow = new_pow
        span *= 2
        yield
    tinv = tinv + _bdot(tinv, npow)
    yield

    wu_b = _bdot(tinv, jnp.concatenate([a_b, av[:R].astype(BF16)], axis=1)).astype(BF16)
    yield
    qo = _bdot(a_rb, wu_b)
    low = jnp.concatenate([jnp.zeros((R, LW), BF16), v_b], axis=1)
    m = _bdot(bkh_t, jnp.concatenate([wu_b, low], axis=0))
    yield
    q_b = (r_s + qo[:, :LW]).astype(BF16)
    o0 = qo[:, LW:] + av[R:]
    li = lax.broadcasted_iota(jnp.int32, (LW, LW), 0)
    lj = lax.broadcasted_iota(jnp.int32, (LW, LW), 1)
    m1_b = (jnp.where(li == lj, jnp.exp(cum_last), 0.0) + m[:, :LW]).astype(BF16)
    return jnp.concatenate([q_b, m1_b], axis=0), o0, m[:, LW:], bonus


def _wkv_phase2(state, ph, c0, n_valid, lanes, g_ref, y_ref, lnw, lnb, bones):
    qm_b, o0, m2, bonus = ph
    C = WKV_CHUNK
    R = WKV_GROUP * C
    qs = _bdot(qm_b, state)
    yield
    o_st = qs[:R] + o0
    new_state = qs[R:] + m2
    o = o_st[0:C, :]
    for h in range(1, WKV_GROUP):
        o = o + o_st[h * C:(h + 1) * C, :]
    mu = _split_dot(o, bones, 2) * (1.0 / HEAD_A)
    yield
    d = o - mu
    var = _split_dot(d * d, bones, 2) * (1.0 / HEAD_A)
    yield
    o = d * lax.rsqrt(var + GN_EPS) * lnw + lnb + bonus
    sl = pl.ds(c0, n_valid)
    y_ref[0, sl, lanes] = (o[:n_valid] * g_ref[0, sl, lanes]).astype(y_ref.dtype)
    return new_state


def _wkv_kernel(r_ref, k_ref, v_ref, ld_ref, a_ref, g_ref, kkw_ref, kaw_ref, rkw_ref, lnw_ref, lnb_ref,
                s0_ref, y_ref, sout_ref, *, T, n_tiles, n_par):
    G, LW, N = WKV_GROUP, WKV_LANES, HEAD_A
    li = lax.broadcasted_iota(jnp.int32, (LW, LW), 0)
    lj = lax.broadcasted_iota(jnp.int32, (LW, LW), 1)
    bones = jnp.where((li // N) == (lj // N), 1.0, 0.0).astype(BF16)
    refs = (r_ref, k_ref, v_ref, ld_ref, a_ref)
    lane_sl = [slice(p * LW, (p + 1) * LW) for p in range(n_tiles)]
    prm = [(kkw_ref[:, s], kaw_ref[:, s], rkw_ref[:, s]) for s in lane_sl]
    post = [(lnw_ref[:, s], lnb_ref[:, s]) for s in lane_sl]

    def steps(c_base, n_valid, n_chunks, states):
        C = WKV_CHUNK
        ph = _interleave([_wkv_phase1(c_base + u * C, n_valid, lane_sl[p], refs, prm[p], bones)
                          for u in range(n_chunks) for p in range(n_tiles)])
        states = list(states)
        for u in range(n_chunks):
            states = _interleave([
                _wkv_phase2(states[p].astype(BF16), ph[u * n_tiles + p], c_base + u * C, n_valid, lane_sl[p],
                            g_ref, y_ref, post[p][0], post[p][1], bones) for p in range(n_tiles)])
        return tuple(states)

    zero = jnp.zeros((N, N), F32)
    states = []
    for p in range(n_tiles):
        blocks = [jnp.concatenate([s0_ref[0, p * G + h].T if j == h else zero for j in range(G)], axis=1)
                  for h in range(G)]
        states.append(jnp.concatenate(blocks, axis=0))
    states = tuple(states)

    head = T % WKV_CHUNK
    if head:
        states = steps(0, head, 1, states)
    n_main = T // WKV_CHUNK
    if n_main:
        def body(i, st):
            return steps(pl.multiple_of(head + i * (n_par * WKV_CHUNK), 16), WKV_CHUNK, n_par, st)

        states = lax.fori_loop(0, n_main // n_par, body, states)

    for p in range(n_tiles):
        for h in range(G):
            sout_ref[0, p * G + h] = states[p][h * N:(h + 1) * N, h * N:(h + 1) * N].T


def _wkv(r, k, v, ld, a, g, kkw, kaw, rkw, lnw, lnb, s0):
    B, T, D = r.shape
    LW, G = WKV_LANES, WKV_GROUP
    n_main = T // WKV_CHUNK
    assert T >= WKV_CHUNK
    n_par = _pick(n_main, (4, 2, 1))
    n_tiles = _pick(D // LW, (2,)) if n_main > 1 else _pick(D // LW, (4, 2))
    bw = n_tiles * LW
    tok = pl.BlockSpec((1, T, bw), lambda b, j: (b, 0, j))
    vec = pl.BlockSpec((1, bw), lambda b, j: (0, j))
    st = pl.BlockSpec((1, n_tiles * G, HEAD_A, HEAD_A), lambda b, j: (b, j, 0, 0))
    y, s_out = pl.pallas_call(
        functools.partial(_wkv_kernel, T=T, n_tiles=n_tiles, n_par=n_par),
        grid=(B, D // bw),
        in_specs=[tok] * 6 + [vec] * 5 + [st],
        out_specs=[tok, st],
        out_shape=[jax.ShapeDtypeStruct((B, T, D), BF16), jax.ShapeDtypeStruct(s0.shape, F32)],
        compiler_params=_cparams(("parallel", "parallel")),
        name="wkv7",
    )(r, k, v, ld, a, g, *[p.reshape(1, D) for p in (kkw, kaw, rkw, lnw, lnb)], s0)
    return y, s_out


def _sb_consts(nk):
    ji = lax.broadcasted_iota(jnp.int32, (nk, nk), 0)
    si = lax.broadcasted_iota(jnp.int32, (nk, nk), 1)
    later = jnp.where(ji > si, -1.0, 0.0).astype(BF16)
    return jnp.concatenate([later, later], axis=0)


def _sb_pair(q, kt, vt, carry, acc, later2, read):
    nq, nk = q.shape[0], kt.shape[0]
    z = _bdot_nt(q, kt) * (HEAD_B ** -0.5)
    yield
    sp = jnp.maximum(z, 0.0) + jnp.log(1.0 + jnp.exp2(jnp.abs(z) * (-LOG2E)))
    spm = sp if read is None else jnp.where(read, sp, 0.0)
    hi, lo = _split(spm, 2)
    if nk % V7X_LANES == 0:
        local = jnp.dot(jnp.concatenate([hi, lo], axis=1), later2, preferred_element_type=F32)
    else:
        local = (jnp.dot(hi, later2[:nk], preferred_element_type=F32)
                 + jnp.dot(lo, later2[:nk], preferred_element_type=F32))
    yield
    total = jnp.broadcast_to(local[:, 0:1] - spm[:, 0:1], (nq, V7X_LANES))
    if nk % V7X_LANES == 0:
        after = local + jnp.concatenate([carry] * (nk // V7X_LANES), axis=1)
    else:
        after = local + carry[:, :nk]
    w = jnp.exp((z - sp) + after)
    if read is not None:
        w = jnp.where(read, w, 0.0)
    return carry + total, acc + _bdot(w, vt)


def _attn_kernel(*refs, T, P, KN, KP):
    if P:
        q_ref, kn_ref, vn_ref, kp_ref, vp_ref, o_ref = refs
    else:
        q_ref, kn_ref, vn_ref, o_ref = refs
    TQ, HB = ATT_TQ, HEAD_B
    n_heads = q_ref.shape[2] // HB
    lanes = [slice(h * HB, (h + 1) * HB) for h in range(n_heads)]
    n_full, rem = T // TQ, T % TQ
    c_new = _sb_consts(KN)
    c_past = _sb_consts(KP) if P else None

    def sweep(qs, ca, n_new):
        def step(k_ref, v_ref, width, consts, jj, ca):
            rows = pl.ds(pl.multiple_of(jj * width, width), width)
            return tuple(_interleave([_sb_pair(qs[h], k_ref[0, rows, lanes[h]], v_ref[0, rows, lanes[h]],
                                               ca[h][0], ca[h][1], consts, None) for h in range(n_heads)]))

        ca = lax.fori_loop(0, n_new, lambda j, c: step(kn_ref, vn_ref, KN, c_new, n_new - 1 - j, c), ca)
        if P:
            n_past = P // KP
            ca = lax.fori_loop(0, n_past, lambda j, c: step(kp_ref, vp_ref, KP, c_past, n_past - 1 - j, c), ca)
        return ca

    def query_tile(q0, nq, k0, nk, off, consts):
        qrows = pl.ds(q0, nq)
        krows = pl.ds(k0, nk)
        qs = [q_ref[0, qrows, lanes[h]] for h in range(n_heads)]
        cmr = lax.broadcasted_iota(jnp.int32, (nq, nk), 1) - lax.broadcasted_iota(jnp.int32, (nq, nk), 0)
        read = cmr < off
        zc = jnp.zeros((nq, V7X_LANES), F32)
        za = jnp.zeros((nq, HB), F32)
        ca = tuple(_interleave([_sb_pair(qs[h], kn_ref[0, krows, lanes[h]], vn_ref[0, krows, lanes[h]],
                                         zc, za, consts, read) for h in range(n_heads)]))
        ca = sweep(qs, ca, k0 // KN)
        for h in range(n_heads):
            o_ref[0, qrows, lanes[h]] = ca[h][1].astype(o_ref.dtype)

    if n_full:
        def full_tile(i, c):
            q0 = pl.multiple_of(i * TQ, TQ)
            k0 = pl.multiple_of((q0 // KN) * KN, KN)
            query_tile(q0, TQ, k0, KN, q0 - k0, c_new)
            return c

        lax.fori_loop(0, n_full, full_tile, 0)
    if rem:
        q0 = n_full * TQ
        query_tile(q0, rem, q0, rem, 0, _sb_consts(rem))


def _attention(q, k_new, v_new, k_past, v_past):
    B, T, D = q.shape
    P = 0 if k_past is None else k_past.shape[1]
    full = T - T % ATT_TQ
    KN = V7X_MXU_DIM if full % V7X_MXU_DIM == 0 and full else ATT_TQ
    KP = V7X_MXU_DIM if P % V7X_MXU_DIM == 0 else ATT_TQ
    assert P % KP == 0 and full % KN == 0
    bw = HEAD_B * _pick(D // HEAD_B, (ATT_HEADS, 2, 1))
    new = pl.BlockSpec((1, T, bw), lambda b, h: (b, 0, h))
    in_specs = [new, new, new]
    args = [q, k_new, v_new]
    if P:
        past = pl.BlockSpec((1, P, bw), lambda b, h: (b, 0, h))
        in_specs += [past, past]
        args += [k_past, v_past]
    return pl.pallas_call(
        functools.partial(_attn_kernel, T=T, P=P, KN=KN, KP=KP),
        grid=(B, D // bw),
        in_specs=in_specs,
        out_specs=new,
        out_shape=jax.ShapeDtypeStruct((B, T, D), BF16),
        compiler_params=_cparams(("parallel", "parallel")),
        name="sb_attention",
    )(*args)


def _ffn_up_kernel(x_ref, wg_ref, wu_ref, st_ref, cw_ref, cb_ref, act_ref, nst_ref, *, T, spt, n_sub):
    tm, tn = act_ref.shape
    rs = tm // n_sub
    wg = _w_chunks(wg_ref)
    wu = _w_chunks(wu_ref)
    cw = cw_ref[...]
    cb = cb_ref[...]
    st = st_ref[...]
    if spt == 1:
        s0, s1 = st[0, 0:1, :], st[0, 1:2, :]
    else:
        s0 = jnp.broadcast_to(st[:, 0:1, :], (spt, T, tn)).reshape(tm, tn)
        s1 = jnp.broadcast_to(st[:, 1:2, :], (spt, T, tn)).reshape(tm, tn)
    lrow = lax.broadcasted_iota(jnp.int32, (rs, 1), 0)
    prev = None
    for c in range(n_sub):
        gate = _dot_chunks(x_ref, slice(c * rs, (c + 1) * rs), wg)
        up = _dot_chunks(x_ref, slice(c * rs, (c + 1) * rs), wu)
        g1 = pltpu.roll(gate, 1, axis=0)
        g2 = pltpu.roll(gate, 2, axis=0)
        if prev is not None:
            g1 = jnp.where(lrow == 0, prev[1:2, :], g1)
            g2 = jnp.where(lrow == 0, prev[0:1, :], jnp.where(lrow == 1, prev[1:2, :], g2))
        pos = (lrow + c * rs) % T
        g1 = jnp.where(pos == 0, s1, g1)
        g2 = jnp.where(pos == 0, s0, jnp.where(pos == 1, s1, g2))
        conv = g2 * cw[0:1, :] + g1 * cw[1:2, :] + gate * cw[2:3, :] + cb
        act_ref[c * rs:(c + 1) * rs, :] = (jax.nn.silu(conv) * up).astype(act_ref.dtype)
        prev = gate[rs - 2:rs, :]
        if spt > 1:
            nst_ref[...] = gate.reshape(spt, T, tn)[:, T - 2:T, :]
    if spt == 1:
        nst_ref[0] = prev


def _ffn_up(x, w_up, layer, conv_state, conv_w, conv_b, T):
    M, D = x.shape
    F = w_up.shape[-1] // 2
    n_seq = M // T
    spt = _pick(n_seq, (8, 4, 2, 1)) if T * 8 <= 512 else 1
    tm = spt * T
    n_sub = 1
    if spt == 1:
        n_sub = next((c for c in (3, 2, 4) if tm % (16 * c) == 0 and tm // c >= 128), 1)
    tn = _pick(F, (256, 128))
    nf = F // tn
    x_mode = pl.Buffered(1) if tm * D * 2 > 8 * 1024 * 1024 else None
    st_spec = pl.BlockSpec((spt, 2, tn), lambda i, j: (i, 0, j))
    return pl.pallas_call(
        functools.partial(_ffn_up_kernel, T=T, spt=spt, n_sub=n_sub),
        grid=(M // tm, nf),
        in_specs=[pl.BlockSpec((tm, D), lambda i, j: (i, 0), pipeline_mode=x_mode),
                  pl.BlockSpec((None, D, tn), lambda i, j: (layer, 0, j)),
                  pl.BlockSpec((None, D, tn), lambda i, j: (layer, 0, j + nf)),
                  st_spec,
                  pl.BlockSpec((CONV_W, tn), lambda i, j: (0, j)),
                  pl.BlockSpec((1, tn), lambda i, j: (0, j))],
        out_specs=[pl.BlockSpec((tm, tn), lambda i, j: (i, j)), st_spec],
        out_shape=[jax.ShapeDtypeStruct((M, F), BF16), jax.ShapeDtypeStruct((n_seq, 2, F), F32)],
        compiler_params=_cparams(("parallel", "arbitrary")),
        name="ffn_up_conv",
    )(x, w_up, w_up, conv_state, conv_w, conv_b.reshape(1, F))


def _run_group(h, shift0, wkv0, conv0, past_k, past_v, p, depth, n_a):
    B, T, D = h.shape
    M = B * T
    flat = lambda t: t.reshape(M, D)
    seq = lambda t: t.reshape(B, T, D)
    gains = p['norm_gains']
    new_wkv, new_shift, new_conv = [], [], []
    v_first = None
    k_new = v_new = k_bf = v_bf = None
    h = flat(h)
    xin = None
    xkv = None
    for layer in range(depth):
        g = gains[layer]
        if layer < n_a:
            i = layer
            mixed_in, last_row = _normmix(seq(h), g[0], shift0[i], p['a_mix'][i])
            xr, xw, xk, xv, xa, xg = (flat(t) for t in mixed_in)
            new_shift.append(last_row)
            r = _matmul(xr, p['a_w_rkv'], widx=(i, 0), out_dtype=F32, name="rwkv_r")
            k = _matmul(xk, p['a_w_rkv'], widx=(i, 1), out_dtype=F32, name="rwkv_k")
            v = _matmul(xv, p['a_w_rkv'], widx=(i, 2), out_dtype=F32, name="rwkv_v")
            ld = _matmul(_matmul(xw, p['a_w1'], widx=(i,), out_dtype=BF16, epilogue=_epi_tanh, name="lora_w1"),
                         p['a_w2'], widx=(i,), out_dtype=F32, epilogue=_epi_log_decay,
                         rows=(p['a_w0'][i].reshape(1, D),), name="lora_w2")
            a = _matmul(_matmul(xa, p['a_a1'], widx=(i,), out_dtype=BF16, name="lora_a1"),
                        p['a_a2'], widx=(i,), out_dtype=F32, epilogue=_epi_bias_sigmoid,
                        rows=(p['a_a0'][i].reshape(1, D),), name="lora_a2")
            gate = _matmul(_matmul(xg, p['a_g1'], widx=(i,), out_dtype=BF16, epilogue=_epi_sigmoid,
                                   name="lora_g1"),
                           p['a_g2'], widx=(i,), out_dtype=F32, name="lora_g2")
            if i == 0:
                v_first = v
            else:
                v = _matmul(_matmul(xv, p['a_v1'], widx=(i - 1,), out_dtype=BF16, name="lora_v1"),
                            p['a_v2'], widx=(i - 1,), out_dtype=F32, epilogue=_epi_value_mix,
                            rows=(p['a_v0'][i - 1].reshape(1, D),), tiles=(v, v_first), name="lora_v2")
            y, s_out = _wkv(seq(r), seq(k), seq(v), seq(ld), seq(a), seq(gate),
                            p['a_k_k'][i], p['a_k_a'][i], p['a_r_k'][i].reshape(D), p['a_ln_w'][i],
                            p['a_ln_b'][i], wkv0[i])
            new_wkv.append(s_out)
            mixed = _matmul(flat(y), p['a_w_out'], widx=(i,), out_dtype=F32, name="rwkv_out")
        else:
            j = layer - n_a
            if layer == n_a:
                k_new, k_bf = _matmul(xkv, p['w_kv'], out_dtype=(F32, BF16), n=D, name="kv_proj_k")
                v_new, v_bf = _matmul(xkv, p['w_kv'], out_dtype=(F32, BF16), n=D, col_off=D, name="kv_proj_v")
            q = _matmul(xin, p['b_w_q'], widx=(j,), out_dtype=BF16, name="attn_q")
            attn = _attention(seq(q), seq(k_bf), seq(v_bf), past_k, past_v)
            mixed = _matmul(flat(attn), p['b_w_out'], widx=(j,), out_dtype=F32, name="attn_out")
        h, (xf,) = _resnorm(h, mixed, g[1], [g[2]])
        act, conv_new = _ffn_up(xf, p['ffn_w_up'], layer, conv0[layer], p['ffn_conv_w'][layer],
                                p['ffn_conv_b'][layer], T)
        new_conv.append(conv_new)
        f = _matmul(act, p['ffn_w_down'], widx=(layer,), out_dtype=F32, name="ffn_down")
        nxt = []
        if layer + 1 < depth and layer + 1 >= n_a:
            nxt.append(gains[layer + 1][0])
            if layer + 1 == n_a:
                nxt.append(p['kv_norm'])
        h, normed = _resnorm(h, f, g[3], nxt)
        if normed:
            xin = normed[0]
            if len(normed) > 1:
                xkv = normed[1]
    return (seq(h), seq(k_new), seq(v_new), jnp.stack(new_wkv), jnp.stack(new_shift),
            jnp.stack(new_conv))


def kernel(x_prompt, x_sample, cache_k, cache_v, state_wkv, state_shift, state_ffn_conv, meta_tokens, norm_gains, a_mix, a_w_rkv, a_w_out, a_w0, a_w1, a_w2, a_a0, a_a1, a_a2, a_v0, a_v1, a_v2, a_g1, a_g2, a_k_k, a_k_a, a_r_k, a_ln_w, a_ln_b, kv_norm, w_kv, b_w_q, b_w_out, ffn_w_up, ffn_conv_w, ffn_conv_b, ffn_w_down):
    B, S, D = x_prompt.shape
    Bs, Ts, _ = x_sample.shape
    depth = norm_gains.shape[0]
    n_a = a_mix.shape[0]
    F = ffn_w_down.shape[1]
    H_a, H_b = D // HEAD_A, D // HEAD_B
    assert n_a >= 1 and depth > n_a
    bf = lambda w: w.astype(BF16)
    p = {
        'norm_gains': norm_gains, 'a_mix': a_mix, 'a_w_rkv': a_w_rkv, 'a_w_out': a_w_out,
        'a_w0': a_w0, 'a_w1': a_w1, 'a_w2': a_w2, 'a_a0': a_a0, 'a_a1': a_a1,
        'a_a2': a_a2, 'a_v0': a_v0, 'a_v1': a_v1, 'a_v2': a_v2, 'a_g1': a_g1,
        'a_g2': a_g2, 'a_k_k': a_k_k, 'a_k_a': a_k_a, 'a_r_k': a_r_k, 'a_ln_w': a_ln_w,
        'a_ln_b': a_ln_b, 'kv_norm': kv_norm, 'w_kv': w_kv, 'b_w_q': b_w_q,
        'b_w_out': b_w_out, 'ffn_w_up': ffn_w_up, 'ffn_conv_w': ffn_conv_w,
        'ffn_conv_b': ffn_conv_b, 'ffn_w_down': bf(ffn_w_down),
    }
    dt = x_prompt.dtype
    T = N_META + S
    meta = jnp.broadcast_to(meta_tokens[None].astype(dt), (B, N_META, D))
    h0 = jnp.concatenate([meta, x_prompt], axis=1)
    h_p, k_p, v_p, wkv_p, shift_p, conv_p = _run_group(
        h0, jnp.zeros((n_a, B, D), dt), jnp.zeros((n_a, B, H_a, HEAD_A, HEAD_A), dt),
        jnp.zeros((depth, B, CONV_W - 1, F), dt), None, None, p, depth, n_a)
    P = cache_k.shape[1]
    h_s, k_s, v_s, wkv_s, shift_s, conv_s = _run_group(
        x_sample, state_shift, state_wkv, state_ffn_conv, bf(cache_k).reshape(Bs, P, D),
        bf(cache_v).reshape(Bs, P, D), p, depth, n_a)
    heads = lambda t: t.reshape(t.shape[0], t.shape[1], H_b, HEAD_B)
    return (h_p[:, N_META:], h_s, heads(k_p), heads(v_p), heads(k_s), heads(v_s),
            wkv_p, wkv_s, shift_p, shift_s, conv_p, conv_s)
```

```python
import functools

import jax
import jax.numpy as jnp
from jax import lax
from jax.experimental import pallas as pl
from jax.experimental.pallas import tpu as pltpu

F32 = jnp.float32
BF16 = jnp.bfloat16

N_META = 16
HEAD_A = 64
HEAD_B = 128
GN_EPS = 64e-5
RMS_EPS = 1e-6
CONV_W = 3

V7X_LANES = 128
V7X_MXU_DIM = 256
V7X_VMEM_BYTES = 64 * 1024 * 1024
VMEM_LIMIT = V7X_VMEM_BYTES - 8 * 1024 * 1024

W_CAST_CHUNK = 1024
KGRID_W_BYTES = 8 * 1024 * 1024
LOG2E = 1.4426950408889634

WKV_GROUP = 2
WKV_LANES = WKV_GROUP * HEAD_A
WKV_CHUNK = 64
ATT_TQ = 128
ATT_HEADS = 8


def _pick(n, prefs):
    for p in prefs:
        if p <= n and n % p == 0:
            return p
    return n


def _cparams(sem):
    return pltpu.CompilerParams(dimension_semantics=sem, vmem_limit_bytes=VMEM_LIMIT)


def _bdot(a, b):
    return jnp.dot(a.astype(BF16), b.astype(BF16), preferred_element_type=F32)


def _bdot_nt(a, b):
    return lax.dot_general(a.astype(BF16), b.astype(BF16), (((1,), (1,)), ((), ())),
                           preferred_element_type=F32)


def _bdot_tn(a, b):
    return lax.dot_general(a.astype(BF16), b.astype(BF16), (((0,), (0,)), ((), ())),
                           preferred_element_type=F32)


def _split(x, parts):
    out = []
    rem = x
    for i in range(parts):
        hi = rem.astype(BF16)
        out.append(hi)
        if i + 1 < parts:
            rem = rem - hi.astype(F32)
    return out


def _split_dot(x, m_bf16, parts):
    acc = None
    for hi in _split(x, parts):
        t = jnp.dot(hi, m_bf16, preferred_element_type=F32)
        acc = t if acc is None else acc + t
    return acc


def _split_rows_dot(xs, m_bf16, parts):
    n = xs[0].shape[0]
    stacked = jnp.concatenate([p for x in xs for p in _split(x, parts)], axis=0)
    res = jnp.dot(stacked, m_bf16, preferred_element_type=F32)
    out = []
    for i in range(len(xs)):
        acc = res[i * parts * n:(i * parts + 1) * n]
        for j in range(1, parts):
            acc = acc + res[(i * parts + j) * n:(i * parts + j + 1) * n]
        out.append(acc)
    return out


def _split_dot_lhs(m_bf16, x, parts):
    acc = None
    for hi in _split(x, parts):
        t = jnp.dot(m_bf16, hi, preferred_element_type=F32)
        acc = t if acc is None else acc + t
    return acc


def _interleave(gens):
    results = [None] * len(gens)
    live = list(range(len(gens)))
    while live:
        still = []
        for i in live:
            try:
                next(gens[i])
                still.append(i)
            except StopIteration as stop:
                results[i] = stop.value
        live = still
    return results


def _rms(x, g):
    return x * lax.rsqrt(jnp.mean(x * x, axis=-1, keepdims=True) + RMS_EPS) * g


def _epi_none(acc):
    return acc


def _epi_tanh(acc):
    return jnp.tanh(acc)


def _epi_sigmoid(acc):
    return jax.nn.sigmoid(acc)


def _epi_bias_sigmoid(acc, bias):
    return jax.nn.sigmoid(bias + acc)


def _epi_log_decay(acc, w0):
    return -jnp.exp(-jax.nn.softplus(-(w0 + acc)) - 0.5)


def _epi_value_mix(acc, v0, v, v_first):
    return v + (v_first - v) * jax.nn.sigmoid(v0 + acc)


def _w_chunks(w_ref):
    K = w_ref.shape[0]
    if w_ref.dtype == BF16 or K % W_CAST_CHUNK or K == W_CAST_CHUNK:
        return [(slice(None), w_ref[...].astype(BF16))]
    return [(slice(k0, k0 + W_CAST_CHUNK), w_ref[k0:k0 + W_CAST_CHUNK, :].astype(BF16))
            for k0 in range(0, K, W_CAST_CHUNK)]


def _dot_chunks(x_ref, rows, chunks):
    acc = None
    for ks, w in chunks:
        t = jnp.dot(x_ref[rows, ks], w, preferred_element_type=F32)
        acc = t if acc is None else acc + t
    return acc


def _dot_cast_w(x_ref, w_ref):
    return _dot_chunks(x_ref, slice(None), _w_chunks(w_ref))


def _mm_kernel(*refs, epilogue, n_row, n_tile):
    x_ref, w_ref = refs[0], refs[1]
    rows = [r[...] for r in refs[2:2 + n_row]]
    tiles = [r[...] for r in refs[2 + n_row:2 + n_row + n_tile]]
    acc = _dot_cast_w(x_ref, w_ref)
    res = epilogue(acc, *rows, *tiles)
    for o_ref in refs[2 + n_row + n_tile:]:
        o_ref[...] = res.astype(o_ref.dtype)


def _matmul_tiles(M, K, N, w_bytes, out_bytes, n_tiles):
    tm = _pick(M, (2064, 1376, 688, 512, 256, 128, 64, 16))
    if K > 8192:
        tm = _pick(M, (688, 512, 256, 128, 64, 16))
    budget = VMEM_LIMIT - 6 * 1024 * 1024
    for tn in (512, 256, 128):
        for x_bufs in (2, 1):
            if N % tn:
                continue
            need = (x_bufs * tm * K * 2 + 2 * K * tn * w_bytes + (W_CAST_CHUNK * tn * 2 if w_bytes > 2 else 0)
                    + 2 * tm * tn * (out_bytes + 4 * n_tiles) + tm * tn * 4)
            if need <= budget:
                return tm, tn, x_bufs
    return tm, _pick(N, (128,)), 1


def _mmk_kernel(*refs, epilogue, n_row, n_tile, n_out):
    x_ref, w_ref = refs[0], refs[1]
    row_refs = refs[2:2 + n_row]
    tile_refs = refs[2 + n_row:2 + n_row + n_tile]
    out_refs = refs[2 + n_row + n_tile:2 + n_row + n_tile + n_out]
    acc_ref = refs[-1]
    k = pl.program_id(0)
    part = jnp.dot(x_ref[...], w_ref[...].astype(BF16), preferred_element_type=F32)

    @pl.when(k == 0)
    def _():
        acc_ref[...] = part

    @pl.when(k > 0)
    def _():
        acc_ref[...] += part

    @pl.when(k == pl.num_programs(0) - 1)
    def _():
        res = epilogue(acc_ref[...], *[r[...] for r in row_refs], *[t[...] for t in tile_refs])
        for o_ref in out_refs:
            o_ref[...] = res.astype(o_ref.dtype)


def _matmul_kgrid(x, w, dtypes, epilogue, rows, tiles, widx, N, col_off, name, as_tuple):
    M, K = x.shape
    tk = _pick(K, tuple(t for t in (2048, 1024, 512, 256, 128) if t * N * w.dtype.itemsize <= KGRID_W_BYTES))
    jblk = col_off // N
    full = pl.BlockSpec((M, N), lambda k: (0, 0))
    in_specs = [pl.BlockSpec((M, tk), lambda k: (0, k)),
                pl.BlockSpec((None,) * len(widx) + (tk, N), lambda k: (*widx, k, jblk))]
    in_specs += [pl.BlockSpec((1, N), lambda k: (0, 0)) for _ in rows] + [full for _ in tiles]
    outs = pl.pallas_call(
        functools.partial(_mmk_kernel, epilogue=epilogue, n_row=len(rows), n_tile=len(tiles), n_out=len(dtypes)),
        grid=(K // tk,),
        in_specs=in_specs,
        out_specs=[full for _ in dtypes],
        out_shape=[jax.ShapeDtypeStruct((M, N), d) for d in dtypes],
        scratch_shapes=[pltpu.VMEM((M, N), F32)],
        compiler_params=_cparams(("arbitrary",)),
        name=name,
    )(x, w, *rows, *tiles)
    return outs if as_tuple else outs[0]


def _matmul(x, w, *, out_dtype, epilogue=_epi_none, rows=(), tiles=(), widx=(), n=None, col_off=0, name):
    M, K = x.shape
    N = w.shape[-1] if n is None else n
    dtypes = out_dtype if isinstance(out_dtype, tuple) else (out_dtype,)
    if M <= 512 and w.dtype == F32 and K * N * 4 >= 2 * KGRID_W_BYTES and col_off % N == 0:
        return _matmul_kgrid(x, w, dtypes, epilogue, rows, tiles, widx, N, col_off, name,
                             isinstance(out_dtype, tuple))
    tm, tn, x_bufs = _matmul_tiles(M, K, N, w.dtype.itemsize, sum(jnp.dtype(d).itemsize for d in dtypes),
                                   len(tiles))
    assert col_off % tn == 0 and w.ndim == 2 + len(widx)
    joff = col_off // tn
    x_mode = pl.Buffered(1) if x_bufs == 1 else None
    in_specs = [pl.BlockSpec((tm, K), lambda i, j: (i, 0), pipeline_mode=x_mode),
                pl.BlockSpec((None,) * len(widx) + (K, tn), lambda i, j: (*widx, 0, j + joff))]
    in_specs += [pl.BlockSpec((1, tn), lambda i, j: (0, j)) for _ in rows]
    in_specs += [pl.BlockSpec((tm, tn), lambda i, j: (i, j)) for _ in tiles]
    outs = pl.pallas_call(
        functools.partial(_mm_kernel, epilogue=epilogue, n_row=len(rows), n_tile=len(tiles)),
        grid=(M // tm, N // tn),
        in_specs=in_specs,
        out_specs=[pl.BlockSpec((tm, tn), lambda i, j: (i, j)) for _ in dtypes],
        out_shape=[jax.ShapeDtypeStruct((M, N), d) for d in dtypes],
        compiler_params=_cparams(("parallel", "arbitrary")),
        name=name,
    )(x, w, *rows, *tiles)
    return outs if isinstance(out_dtype, tuple) else outs[0]


def _resnorm_kernel(*refs, n_next):
    h_ref, y_ref, gp_ref = refs[0], refs[1], refs[2]
    gn_refs = refs[3:3 + n_next]
    hn_ref = refs[3 + n_next]
    xo_refs = refs[4 + n_next:]
    hn = h_ref[...] + _rms(y_ref[...], gp_ref[...])
    hn_ref[...] = hn
    for gn_ref, xo_ref in zip(gn_refs, xo_refs):
        xo_ref[...] = _rms(hn, gn_ref[...]).astype(xo_ref.dtype)


def _resnorm(h, y, g_post, g_next):
    M, D = h.shape
    tr = _pick(M, (192, 128, 64, 16))
    n = len(g_next)
    row = pl.BlockSpec((tr, D), lambda i: (i, 0))
    vec = pl.BlockSpec((1, D), lambda i: (0, 0))
    outs = pl.pallas_call(
        functools.partial(_resnorm_kernel, n_next=n),
        grid=(M // tr,),
        in_specs=[row, row, vec] + [vec] * n,
        out_specs=[row] + [row] * n,
        out_shape=[jax.ShapeDtypeStruct((M, D), F32)] + [jax.ShapeDtypeStruct((M, D), BF16)] * n,
        compiler_params=_cparams(("parallel",)),
        name="resnorm",
    )(h, y, g_post.reshape(1, D), *[g.reshape(1, D) for g in g_next])
    return outs[0], list(outs[1:])


def _normmix_kernel(h_ref, halo_ref, g_ref, shift_ref, mix_ref, *out_refs, tt):
    t = pl.program_id(1)
    g = g_ref[...]
    xn = _rms(h_ref[0], g)
    halo = _rms(halo_ref[0], g)
    prev_row = jnp.where(t == 0, shift_ref[0], halo[7:8, :])
    row = lax.broadcasted_iota(jnp.int32, (tt, 1), 0)
    x_prev = jnp.where(row == 0, prev_row, pltpu.roll(xn, 1, axis=0))
    xx = x_prev - xn
    for i in range(6):
        out_refs[i][0] = (xn + xx * mix_ref[i:i + 1, :]).astype(BF16)
    out_refs[6][0] = xn[tt - 1:tt, :]


def _normmix(h, g, shift0, mix):
    B, T, D = h.shape
    tt = _pick(T, (64, 48, 32, 16))
    nb = tt // 8
    tok = pl.BlockSpec((1, tt, D), lambda b, t: (b, t, 0))
    outs = pl.pallas_call(
        functools.partial(_normmix_kernel, tt=tt),
        grid=(B, T // tt),
        in_specs=[tok,
                  pl.BlockSpec((1, 8, D), lambda b, t: (b, jnp.maximum(t * nb - 1, 0), 0)),
                  pl.BlockSpec((1, D), lambda b, t: (0, 0)),
                  pl.BlockSpec((1, 1, D), lambda b, t: (b, 0, 0)),
                  pl.BlockSpec((6, D), lambda b, t: (0, 0))],
        out_specs=[tok] * 6 + [pl.BlockSpec((1, 1, D), lambda b, t: (b, 0, 0))],
        out_shape=[jax.ShapeDtypeStruct((B, T, D), BF16)] * 6 + [jax.ShapeDtypeStruct((B, 1, D), F32)],
        compiler_params=_cparams(("parallel", "arbitrary")),
        name="normmix",
    )(h, h, g.reshape(1, D), shift0.reshape(B, 1, D), mix)
    return list(outs[:6]), outs[6].reshape(B, D)


def _wkv_phase1(c0, n_valid, lanes, refs, prm, bones):
    r_ref, k_ref, v_ref, ld_ref, a_ref = refs
    kkw, kaw, rkw = prm
    G, LW, C = WKV_GROUP, WKV_LANES, WKV_CHUNK
    R = G * C
    sl = pl.ds(c0, C)
    r = r_ref[0, sl, lanes]
    k = k_ref[0, sl, lanes]
    v = v_ref[0, sl, lanes]
    ld = ld_ref[0, sl, lanes]
    a = a_ref[0, sl, lanes]
    if n_valid < C:
        valid = lax.broadcasted_iota(jnp.int32, (C, 1), 0) < n_valid
        r, k, v, ld = (jnp.where(valid, x, 0.0) for x in (r, k, v, ld))

    kk = k * kkw
    km = k * (1.0 + (a - 1.0) * kaw)
    ti = lax.broadcasted_iota(jnp.int32, (C, C), 0)
    si = lax.broadcasted_iota(jnp.int32, (C, C), 1)
    sums = _split_rows_dot([kk * kk, r * km * rkw], bones, 2)
    kk_ss, rk_sum = sums[0], sums[1]
    tri = jnp.where(ti >= si, 1.0, 0.0).astype(BF16)
    cum3 = jnp.dot(tri, jnp.concatenate(_split(ld, 3), axis=1), preferred_element_type=F32)
    cum = cum3[:, :LW] + cum3[:, LW:2 * LW] + cum3[:, 2 * LW:]
    yield
    kk = kk / jnp.maximum(jnp.sqrt(kk_ss), 1e-12)
    bonus = rk_sum * v
    b = kk * a
    cum_last = cum[C - 1:C, :]
    e_neg = jnp.exp(-cum)
    e_rem = jnp.exp(cum_last - cum)
    lane_head = lax.broadcasted_iota(jnp.int32, (1, LW), 1) // HEAD_A

    def stack(x):
        return jnp.concatenate([jnp.where(lane_head == h, x, 0.0) for h in range(G)], axis=0)

    r_s = stack(jnp.exp(cum) * r)
    a_b = stack(jnp.exp(cum - ld) * (-kk)).astype(BF16)
    v_b = stack(v).astype(BF16)
    ar_b = jnp.concatenate([a_b, r_s.astype(BF16)], axis=0)
    bk_b = jnp.concatenate([stack(e_neg * b), stack(e_neg * km)], axis=0).astype(BF16)
    bkh_t = jnp.concatenate([stack(e_rem * b).T, stack(e_rem * km).T], axis=1).astype(BF16)

    big = _bdot_nt(ar_b, bk_b)
    yield
    row = lax.broadcasted_iota(jnp.int32, (R, R), 0)
    col = lax.broadcasted_iota(jnp.int32, (R, R), 1)
    same = (row // C) == (col // C)
    strict = same & (row > col)
    incl = same & (row >= col)
    n_ab = jnp.where(strict, big[:R, :R], 0.0)
    a_ak = jnp.where(strict, big[:R, R:], 0.0)
    a_rb = jnp.where(incl, big[R:, :R], 0.0).astype(BF16)
    a_rk = jnp.where(incl, big[R:, R:], 0.0)
    av = _bdot(jnp.concatenate([a_ak, a_rk], axis=0), v_b)

    tinv = jnp.where(row == col, 1.0, 0.0) + n_ab
    npow = n_ab
    span = 2
    while span < C:
        npow_b = npow.astype(BF16)
        if span > 2:
            both = _bdot(jnp.concatenate([npow_b, tinv.astype(BF16)], axis=0), npow_b)
            npow, tinv = both[:R], tinv + both[R:]
        else:
            npow = _bdot(npow_b, npow_b)
        span *= 2
        yield
    tinv = tinv + _bdot(tinv, npow)
    yield

    wu_b = _bdot(tinv, jnp.concatenate([a_b, av[:R].astype(BF16)], axis=1)).astype(BF16)
    yield
    qo = _bdot(a_rb, wu_b)
    low = jnp.concatenate([jnp.zeros((R, LW), BF16), v_b], axis=1)
    m = _bdot(bkh_t, jnp.concatenate([wu_b, low], axis=0))
    yield
    q_b = (r_s + qo[:, :LW]).astype(BF16)
    o0 = qo[:, LW:] + av[R:]
    li = lax.broadcasted_iota(jnp.int32, (LW, LW), 0)
    lj = lax.broadcasted_iota(jnp.int32, (LW, LW), 1)
    m1_b = (jnp.where(li == lj, jnp.exp(cum_last), 0.0) + m[:, :LW]).astype(BF16)
    return jnp.concatenate([q_b, m1_b], axis=0), o0, m[:, LW:], bonus


def _wkv_phase2(state, ph, c0, n_valid, lanes, g_ref, y_ref, lnw, lnb, bones):
    qm_b, o0, m2, bonus = ph
    C = WKV_CHUNK
    R = WKV_GROUP * C
    qs = _bdot(qm_b, state)
    yield
    o_st = qs[:R] + o0
    new_state = qs[R:] + m2
    o = o_st[0:C, :]
    for h in range(1, WKV_GROUP):
        o = o + o_st[h * C:(h + 1) * C, :]
    mu = _split_rows_dot([o], bones, 2)[0] * (1.0 / HEAD_A)
    yield
    d = o - mu
    var = _split_rows_dot([d * d], bones, 2)[0] * (1.0 / HEAD_A)
    yield
    o = d * lax.rsqrt(var + GN_EPS) * lnw + lnb + bonus
    sl = pl.ds(c0, n_valid)
    y_ref[0, sl, lanes] = (o[:n_valid] * g_ref[0, sl, lanes]).astype(y_ref.dtype)
    return new_state


def _wkv_kernel(r_ref, k_ref, v_ref, ld_ref, a_ref, g_ref, kkw_ref, kaw_ref, rkw_ref, lnw_ref, lnb_ref,
                s0_ref, y_ref, sout_ref, *, T, n_tiles, n_par):
    G, LW, N = WKV_GROUP, WKV_LANES, HEAD_A
    li = lax.broadcasted_iota(jnp.int32, (LW, LW), 0)
    lj = lax.broadcasted_iota(jnp.int32, (LW, LW), 1)
    bones = jnp.where((li // N) == (lj // N), 1.0, 0.0).astype(BF16)
    refs = (r_ref, k_ref, v_ref, ld_ref, a_ref)
    lane_sl = [slice(p * LW, (p + 1) * LW) for p in range(n_tiles)]
    prm = [(kkw_ref[:, s], kaw_ref[:, s], rkw_ref[:, s]) for s in lane_sl]
    post = [(lnw_ref[:, s], lnb_ref[:, s]) for s in lane_sl]

    def steps(c_base, n_valid, n_chunks, states):
        C = WKV_CHUNK
        ph = _interleave([_wkv_phase1(c_base + u * C, n_valid, lane_sl[p], refs, prm[p], bones)
                          for u in range(n_chunks) for p in range(n_tiles)])
        states = list(states)
        for u in range(n_chunks):
            states = _interleave([
                _wkv_phase2(states[p].astype(BF16), ph[u * n_tiles + p], c_base + u * C, n_valid, lane_sl[p],
                            g_ref, y_ref, post[p][0], post[p][1], bones) for p in range(n_tiles)])
        return tuple(states)

    zero = jnp.zeros((N, N), F32)
    states = []
    for p in range(n_tiles):
        blocks = [jnp.concatenate([s0_ref[0, p * G + h].T if j == h else zero for j in range(G)], axis=1)
                  for h in range(G)]
        states.append(jnp.concatenate(blocks, axis=0))
    states = tuple(states)

    head = T % WKV_CHUNK
    if head:
        states = steps(0, head, 1, states)
    n_main = T // WKV_CHUNK
    if n_main:
        def body(i, st):
            return steps(pl.multiple_of(head + i * (n_par * WKV_CHUNK), 16), WKV_CHUNK, n_par, st)

        states = lax.fori_loop(0, n_main // n_par, body, states)

    for p in range(n_tiles):
        for h in range(G):
            sout_ref[0, p * G + h] = states[p][h * N:(h + 1) * N, h * N:(h + 1) * N].T


def _wkv(r, k, v, ld, a, g, kkw, kaw, rkw, lnw, lnb, s0):
    B, T, D = r.shape
    LW, G = WKV_LANES, WKV_GROUP
    n_main = T // WKV_CHUNK
    assert T >= WKV_CHUNK
    n_par = _pick(n_main, (4, 2, 1))
    n_tiles = _pick(D // LW, (2,)) if n_main > 1 else _pick(D // LW, (8, 4, 2))
    bw = n_tiles * LW
    tok = pl.BlockSpec((1, T, bw), lambda b, j: (b, 0, j))
    vec = pl.BlockSpec((1, bw), lambda b, j: (0, j))
    st = pl.BlockSpec((1, n_tiles * G, HEAD_A, HEAD_A), lambda b, j: (b, j, 0, 0))
    y, s_out = pl.pallas_call(
        functools.partial(_wkv_kernel, T=T, n_tiles=n_tiles, n_par=n_par),
        grid=(B, D // bw),
        in_specs=[tok] * 6 + [vec] * 5 + [st],
        out_specs=[tok, st],
        out_shape=[jax.ShapeDtypeStruct((B, T, D), BF16), jax.ShapeDtypeStruct(s0.shape, F32)],
        compiler_params=_cparams(("parallel", "parallel")),
        name="wkv7",
    )(r, k, v, ld, a, g, *[p.reshape(1, D) for p in (kkw, kaw, rkw, lnw, lnb)], s0)
    return y, s_out


def _sb_consts(nk):
    ji = lax.broadcasted_iota(jnp.int32, (nk, nk), 0)
    si = lax.broadcasted_iota(jnp.int32, (nk, nk), 1)
    later = jnp.where(ji > si, -1.0, 0.0).astype(BF16)
    return jnp.concatenate([later, later], axis=0)


def _sb_pair(q, kt, vt, carry, acc, later2, read):
    nq, nk = q.shape[0], kt.shape[0]
    z = _bdot_nt(q, kt) * (HEAD_B ** -0.5)
    yield
    sp = jnp.maximum(z, 0.0) + jnp.log(1.0 + jnp.exp2(jnp.abs(z) * (-LOG2E)))
    spm = sp if read is None else jnp.where(read, sp, 0.0)
    hi, lo = _split(spm, 2)
    if nk % V7X_LANES == 0:
        local = jnp.dot(jnp.concatenate([hi, lo], axis=1), later2, preferred_element_type=F32)
    else:
        local = (jnp.dot(hi, later2[:nk], preferred_element_type=F32)
                 + jnp.dot(lo, later2[:nk], preferred_element_type=F32))
    yield
    total = jnp.broadcast_to(local[:, 0:1] - spm[:, 0:1], (nq, V7X_LANES))
    if nk % V7X_LANES == 0:
        after = local + jnp.concatenate([carry] * (nk // V7X_LANES), axis=1)
    else:
        after = local + carry[:, :nk]
    w = jnp.exp((z - sp) + after)
    if read is not None:
        w = jnp.where(read, w, 0.0)
    return carry + total, acc + _bdot(w, vt)


def _attn_kernel(*refs, T, P, KN, KP):
    if P:
        q_ref, kn_ref, vn_ref, kp_ref, vp_ref, o_ref = refs
    else:
        q_ref, kn_ref, vn_ref, o_ref = refs
    TQ, HB = ATT_TQ, HEAD_B
    n_heads = q_ref.shape[2] // HB
    lanes = [slice(h * HB, (h + 1) * HB) for h in range(n_heads)]
    n_full, rem = T // TQ, T % TQ
    c_new = _sb_consts(KN)
    c_past = _sb_consts(KP) if P else None

    def sweep(qs, ca, n_new):
        def step(k_ref, v_ref, width, consts, jj, ca):
            rows = pl.ds(pl.multiple_of(jj * width, width), width)
            return tuple(_interleave([_sb_pair(qs[h], k_ref[0, rows, lanes[h]], v_ref[0, rows, lanes[h]],
                                               ca[h][0], ca[h][1], consts, None) for h in range(n_heads)]))

        ca = lax.fori_loop(0, n_new, lambda j, c: step(kn_ref, vn_ref, KN, c_new, n_new - 1 - j, c), ca)
        if P:
            n_past = P // KP
            ca = lax.fori_loop(0, n_past, lambda j, c: step(kp_ref, vp_ref, KP, c_past, n_past - 1 - j, c), ca)
        return ca

    def query_tile(q0, nq, k0, nk, off, consts):
        qrows = pl.ds(q0, nq)
        krows = pl.ds(k0, nk)
        qs = [q_ref[0, qrows, lanes[h]] for h in range(n_heads)]
        cmr = lax.broadcasted_iota(jnp.int32, (nq, nk), 1) - lax.broadcasted_iota(jnp.int32, (nq, nk), 0)
        read = cmr < off
        zc = jnp.zeros((nq, V7X_LANES), F32)
        za = jnp.zeros((nq, HB), F32)
        ca = tuple(_interleave([_sb_pair(qs[h], kn_ref[0, krows, lanes[h]], vn_ref[0, krows, lanes[h]],
                                         zc, za, consts, read) for h in range(n_heads)]))
        ca = sweep(qs, ca, k0 // KN)
        for h in range(n_heads):
            o_ref[0, qrows, lanes[h]] = ca[h][1].astype(o_ref.dtype)

    if n_full:
        def full_tile(i, c):
            q0 = pl.multiple_of(i * TQ, TQ)
            k0 = pl.multiple_of((q0 // KN) * KN, KN)
            query_tile(q0, TQ, k0, KN, q0 - k0, c_new)
            return c

        lax.fori_loop(0, n_full, full_tile, 0)
    if rem:
        q0 = n_full * TQ
        query_tile(q0, rem, q0, rem, 0, _sb_consts(rem))


def _attention(q, k_new, v_new, k_past, v_past):
    B, T, D = q.shape
    P = 0 if k_past is None else k_past.shape[1]
    full = T - T % ATT_TQ
    KN = V7X_MXU_DIM if full % V7X_MXU_DIM == 0 and full else ATT_TQ
    KP = V7X_MXU_DIM if P % V7X_MXU_DIM == 0 else ATT_TQ
    assert P % KP == 0 and full % KN == 0
    bw = HEAD_B * _pick(D // HEAD_B, (ATT_HEADS, 2, 1))
    new = pl.BlockSpec((1, T, bw), lambda b, h: (b, 0, h))
    in_specs = [new, new, new]
    args = [q, k_new, v_new]
    if P:
        past = pl.BlockSpec((1, P, bw), lambda b, h: (b, 0, h))
        in_specs += [past, past]
        args += [k_past, v_past]
    return pl.pallas_call(
        functools.partial(_attn_kernel, T=T, P=P, KN=KN, KP=KP),
        grid=(B, D // bw),
        in_specs=in_specs,
        out_specs=new,
        out_shape=jax.ShapeDtypeStruct((B, T, D), BF16),
        compiler_params=_cparams(("parallel", "parallel")),
        name="sb_attention",
    )(*args)


def _ffn_up_kernel(x_ref, wg_ref, wu_ref, st_ref, cw_ref, cb_ref, act_ref, nst_ref, *, T, spt, subs):
    tm, tn = act_ref.shape
    wg = _w_chunks(wg_ref)
    wu = _w_chunks(wu_ref)
    cw = cw_ref[...]
    cb = cb_ref[...]
    st = st_ref[...]
    if spt == 1:
        s0, s1 = st[0, 0:1, :], st[0, 1:2, :]
    else:
        s0 = jnp.broadcast_to(st[:, 0:1, :], (spt, T, tn)).reshape(tm, tn)
        s1 = jnp.broadcast_to(st[:, 1:2, :], (spt, T, tn)).reshape(tm, tn)
    prev = None
    r0 = 0
    for rs in subs:
        rows = slice(r0, r0 + rs)
        lrow = lax.broadcasted_iota(jnp.int32, (rs, 1), 0)
        gate = _dot_chunks(x_ref, rows, wg)
        up = _dot_chunks(x_ref, rows, wu)
        g1 = pltpu.roll(gate, 1, axis=0)
        g2 = pltpu.roll(gate, 2, axis=0)
        if prev is not None:
            g1 = jnp.where(lrow == 0, prev[1:2, :], g1)
            g2 = jnp.where(lrow == 0, prev[0:1, :], jnp.where(lrow == 1, prev[1:2, :], g2))
        pos = (lrow + r0) % T
        g1 = jnp.where(pos == 0, s1, g1)
        g2 = jnp.where(pos == 0, s0, jnp.where(pos == 1, s1, g2))
        conv = g2 * cw[0:1, :] + g1 * cw[1:2, :] + gate * cw[2:3, :] + cb
        act_ref[rows, :] = (jax.nn.silu(conv) * up).astype(act_ref.dtype)
        prev = gate[rs - 2:rs, :]
        r0 += rs
        if spt > 1:
            nst_ref[...] = gate.reshape(spt, T, tn)[:, T - 2:T, :]
    if spt == 1:
        nst_ref[0] = prev


def _ffn_up(x, w_up, layer, conv_state, conv_w, conv_b, T):
    M, D = x.shape
    F = w_up.shape[-1] // 2
    n_seq = M // T
    spt = _pick(n_seq, (8, 4, 2, 1)) if T * 8 <= 512 else 1
    tm = spt * T
    subs = [tm]
    if spt == 1 and tm >= 512 and tm % 16 == 0:
        subs, rem = [], tm
        while rem > 640:
            subs.append(512)
            rem -= 512
        subs += [rem - 128, 128] if rem > 256 else [rem]
    tn = _pick(F, (256, 128))
    nf = F // tn
    x_mode = pl.Buffered(1) if tm * D * 2 > 8 * 1024 * 1024 else None
    st_spec = pl.BlockSpec((spt, 2, tn), lambda i, j: (i, 0, j))
    return pl.pallas_call(
        functools.partial(_ffn_up_kernel, T=T, spt=spt, subs=tuple(subs)),
        grid=(M // tm, nf),
        in_specs=[pl.BlockSpec((tm, D), lambda i, j: (i, 0), pipeline_mode=x_mode),
                  pl.BlockSpec((None, D, tn), lambda i, j: (layer, 0, j)),
                  pl.BlockSpec((None, D, tn), lambda i, j: (layer, 0, j + nf)),
                  st_spec,
                  pl.BlockSpec((CONV_W, tn), lambda i, j: (0, j)),
                  pl.BlockSpec((1, tn), lambda i, j: (0, j))],
        out_specs=[pl.BlockSpec((tm, tn), lambda i, j: (i, j)), st_spec],
        out_shape=[jax.ShapeDtypeStruct((M, F), BF16), jax.ShapeDtypeStruct((n_seq, 2, F), F32)],
        compiler_params=_cparams(("parallel", "arbitrary")),
        name="ffn_up_conv",
    )(x, w_up, w_up, conv_state, conv_w, conv_b.reshape(1, F))


def _run_group(h, shift0, wkv0, conv0, past_k, past_v, p, depth, n_a):
    B, T, D = h.shape
    M = B * T
    flat = lambda t: t.reshape(M, D)
    seq = lambda t: t.reshape(B, T, D)
    gains = p['norm_gains']
    new_wkv, new_shift, new_conv = [], [], []
    v_first = None
    k_new = v_new = k_bf = v_bf = None
    h = flat(h)
    xin = None
    xkv = None
    for layer in range(depth):
        g = gains[layer]
        if layer < n_a:
            i = layer
            mixed_in, last_row = _normmix(seq(h), g[0], shift0[i], p['a_mix'][i])
            xr, xw, xk, xv, xa, xg = (flat(t) for t in mixed_in)
            new_shift.append(last_row)
            r = _matmul(xr, p['a_w_rkv'], widx=(i, 0), out_dtype=F32, name="rwkv_r")
            k = _matmul(xk, p['a_w_rkv'], widx=(i, 1), out_dtype=F32, name="rwkv_k")
            v = _matmul(xv, p['a_w_rkv'], widx=(i, 2), out_dtype=F32, name="rwkv_v")
            ld = _matmul(_matmul(xw, p['a_w1'], widx=(i,), out_dtype=BF16, epilogue=_epi_tanh, name="lora_w1"),
                         p['a_w2'], widx=(i,), out_dtype=F32, epilogue=_epi_log_decay,
                         rows=(p['a_w0'][i].reshape(1, D),), name="lora_w2")
            a = _matmul(_matmul(xa, p['a_a1'], widx=(i,), out_dtype=BF16, name="lora_a1"),
                        p['a_a2'], widx=(i,), out_dtype=F32, epilogue=_epi_bias_sigmoid,
                        rows=(p['a_a0'][i].reshape(1, D),), name="lora_a2")
            gate = _matmul(_matmul(xg, p['a_g1'], widx=(i,), out_dtype=BF16, epilogue=_epi_sigmoid,
                                   name="lora_g1"),
                           p['a_g2'], widx=(i,), out_dtype=F32, name="lora_g2")
            if i == 0:
                v_first = v
            else:
                v = _matmul(_matmul(xv, p['a_v1'], widx=(i - 1,), out_dtype=BF16, name="lora_v1"),
                            p['a_v2'], widx=(i - 1,), out_dtype=F32, epilogue=_epi_value_mix,
                            rows=(p['a_v0'][i - 1].reshape(1, D),), tiles=(v, v_first), name="lora_v2")
            y, s_out = _wkv(seq(r), seq(k), seq(v), seq(ld), seq(a), seq(gate),
                            p['a_k_k'][i], p['a_k_a'][i], p['a_r_k'][i].reshape(D), p['a_ln_w'][i],
                            p['a_ln_b'][i], wkv0[i])
            new_wkv.append(s_out)
            mixed = _matmul(flat(y), p['a_w_out'], widx=(i,), out_dtype=F32, name="rwkv_out")
        else:
            j = layer - n_a
            if layer == n_a:
                k_new, k_bf = _matmul(xkv, p['w_kv'], out_dtype=(F32, BF16), n=D, name="kv_proj_k")
                v_new, v_bf = _matmul(xkv, p['w_kv'], out_dtype=(F32, BF16), n=D, col_off=D, name="kv_proj_v")
            q = _matmul(xin, p['b_w_q'], widx=(j,), out_dtype=BF16, name="attn_q")
            attn = _attention(seq(q), seq(k_bf), seq(v_bf), past_k, past_v)
            mixed = _matmul(flat(attn), p['b_w_out'], widx=(j,), out_dtype=F32, name="attn_out")
        h, (xf,) = _resnorm(h, mixed, g[1], [g[2]])
        act, conv_new = _ffn_up(xf, p['ffn_w_up'], layer, conv0[layer], p['ffn_conv_w'][layer],
                                p['ffn_conv_b'][layer], T)
        new_conv.append(conv_new)
        f = _matmul(act, p['ffn_w_down'], widx=(layer,), out_dtype=F32, name="ffn_down")
        nxt = []
        if layer + 1 < depth and layer + 1 >= n_a:
            nxt.append(gains[layer + 1][0])
            if layer + 1 == n_a:
                nxt.append(p['kv_norm'])
        h, normed = _resnorm(h, f, g[3], nxt)
        if normed:
            xin = normed[0]
            if len(normed) > 1:
                xkv = normed[1]
    return (seq(h), seq(k_new), seq(v_new), jnp.stack(new_wkv), jnp.stack(new_shift),
            jnp.stack(new_conv))


def kernel(x_prompt, x_sample, cache_k, cache_v, state_wkv, state_shift, state_ffn_conv, meta_tokens, norm_gains, a_mix, a_w_rkv, a_w_out, a_w0, a_w1, a_w2, a_a0, a_a1, a_a2, a_v0, a_v1, a_v2, a_g1, a_g2, a_k_k, a_k_a, a_r_k, a_ln_w, a_ln_b, kv_norm, w_kv, b_w_q, b_w_out, ffn_w_up, ffn_conv_w, ffn_conv_b, ffn_w_down):
    B, S, D = x_prompt.shape
    Bs, Ts, _ = x_sample.shape
    depth = norm_gains.shape[0]
    n_a = a_mix.shape[0]
    F = ffn_w_down.shape[1]
    H_a, H_b = D // HEAD_A, D // HEAD_B
    assert n_a >= 1 and depth > n_a
    bf = lambda w: w.astype(BF16)
    p = {
        'norm_gains': norm_gains, 'a_mix': a_mix, 'a_w_rkv': a_w_rkv, 'a_w_out': a_w_out,
        'a_w0': a_w0, 'a_w1': a_w1, 'a_w2': a_w2, 'a_a0': a_a0, 'a_a1': a_a1,
        'a_a2': a_a2, 'a_v0': a_v0, 'a_v1': a_v1, 'a_v2': a_v2, 'a_g1': a_g1,
        'a_g2': a_g2, 'a_k_k': a_k_k, 'a_k_a': a_k_a, 'a_r_k': a_r_k, 'a_ln_w': a_ln_w,
        'a_ln_b': a_ln_b, 'kv_norm': kv_norm, 'w_kv': w_kv, 'b_w_q': b_w_q,
        'b_w_out': b_w_out, 'ffn_w_up': ffn_w_up, 'ffn_conv_w': ffn_conv_w,
        'ffn_conv_b': ffn_conv_b, 'ffn_w_down': bf(ffn_w_down),
    }
    dt = x_prompt.dtype
    T = N_META + S
    meta = jnp.broadcast_to(meta_tokens[None].astype(dt), (B, N_META, D))
    h0 = jnp.concatenate([meta, x_prompt], axis=1)
    h_p, k_p, v_p, wkv_p, shift_p, conv_p = _run_group(
        h0, jnp.zeros((n_a, B, D), dt), jnp.zeros((n_a, B, H_a, HEAD_A, HEAD_A), dt),
        jnp.zeros((depth, B, CONV_W - 1, F), dt), None, None, p, depth, n_a)
    P = cache_k.shape[1]
    h_s, k_s, v_s, wkv_s, shift_s, conv_s = _run_group(
        x_sample, state_shift, state_wkv, state_ffn_conv, bf(cache_k).reshape(Bs, P, D),
        bf(cache_v).reshape(Bs, P, D), p, depth, n_a)
    heads = lambda t: t.reshape(t.shape[0], t.shape[1], H_b, HEAD_B)
    return (h_p[:, N_META:], h_s, heads(k_p), heads(v_p), heads(k_s), heads(v_s),
            wkv_p, wkv_s, shift_p, shift_s, conv_p, conv_s)
```

```python
import functools

import jax
import jax.numpy as jnp
from jax import lax
from jax.experimental import pallas as pl
from jax.experimental.pallas import tpu as pltpu

F32 = jnp.float32
BF16 = jnp.bfloat16

N_META = 16
HEAD_A = 64
HEAD_B = 128
GN_EPS = 64e-5
RMS_EPS = 1e-6
CONV_W = 3

V7X_LANES = 128
V7X_MXU_DIM = 256
V7X_VMEM_BYTES = 64 * 1024 * 1024
VMEM_LIMIT = V7X_VMEM_BYTES - 8 * 1024 * 1024

W_CAST_CHUNK = 1024
KGRID_W_BYTES = 8 * 1024 * 1024
LOG2E = 1.4426950408889634

WKV_GROUP = 2
WKV_LANES = WKV_GROUP * HEAD_A
WKV_CHUNK = 64
ATT_TQ = 128
ATT_HEADS = 8


def _pick(n, prefs):
    for p in prefs:
        if p <= n and n % p == 0:
            return p
    return n


def _cparams(sem):
    return pltpu.CompilerParams(dimension_semantics=sem, vmem_limit_bytes=VMEM_LIMIT)


def _bdot(a, b):
    return jnp.dot(a.astype(BF16), b.astype(BF16), preferred_element_type=F32)


def _bdot_nt(a, b):
    return lax.dot_general(a.astype(BF16), b.astype(BF16), (((1,), (1,)), ((), ())),
                           preferred_element_type=F32)


def _bdot_tn(a, b):
    return lax.dot_general(a.astype(BF16), b.astype(BF16), (((0,), (0,)), ((), ())),
                           preferred_element_type=F32)


def _split(x, parts):
    out = []
    rem = x
    for i in range(parts):
        hi = rem.astype(BF16)
        out.append(hi)
        if i + 1 < parts:
            rem = rem - hi.astype(F32)
    return out


def _split_dot(x, m_bf16, parts):
    acc = None
    for hi in _split(x, parts):
        t = jnp.dot(hi, m_bf16, preferred_element_type=F32)
        acc = t if acc is None else acc + t
    return acc


def _split_rows_dot(xs, m_bf16, parts):
    n = xs[0].shape[0]
    stacked = jnp.concatenate([p for x in xs for p in _split(x, parts)], axis=0)
    res = jnp.dot(stacked, m_bf16, preferred_element_type=F32)
    out = []
    for i in range(len(xs)):
        acc = res[i * parts * n:(i * parts + 1) * n]
        for j in range(1, parts):
            acc = acc + res[(i * parts + j) * n:(i * parts + j + 1) * n]
        out.append(acc)
    return out


def _split_dot_lhs(m_bf16, x, parts):
    acc = None
    for hi in _split(x, parts):
        t = jnp.dot(m_bf16, hi, preferred_element_type=F32)
        acc = t if acc is None else acc + t
    return acc


def _interleave(gens):
    results = [None] * len(gens)
    live = list(range(len(gens)))
    while live:
        still = []
        for i in live:
            try:
                next(gens[i])
                still.append(i)
            except StopIteration as stop:
                results[i] = stop.value
        live = still
    return results


def _rms(x, g):
    return x * lax.rsqrt(jnp.mean(x * x, axis=-1, keepdims=True) + RMS_EPS) * g


def _epi_none(acc):
    return acc


def _epi_tanh(acc):
    return jnp.tanh(acc)


def _epi_sigmoid(acc):
    return jax.nn.sigmoid(acc)


def _epi_bias_sigmoid(acc, bias):
    return jax.nn.sigmoid(bias + acc)


def _epi_log_decay(acc, w0):
    return -jnp.exp(-jax.nn.softplus(-(w0 + acc)) - 0.5)


def _epi_value_mix(acc, v0, v, v_first):
    return v + (v_first - v) * jax.nn.sigmoid(v0 + acc)


def _w_chunks(w_ref):
    K = w_ref.shape[0]
    if w_ref.dtype == BF16 or K % W_CAST_CHUNK or K == W_CAST_CHUNK:
        return [(slice(None), w_ref[...].astype(BF16))]
    return [(slice(k0, k0 + W_CAST_CHUNK), w_ref[k0:k0 + W_CAST_CHUNK, :].astype(BF16))
            for k0 in range(0, K, W_CAST_CHUNK)]


def _dot_chunks(x_ref, rows, chunks):
    acc = None
    for ks, w in chunks:
        t = jnp.dot(x_ref[rows, ks], w, preferred_element_type=F32)
        acc = t if acc is None else acc + t
    return acc


def _dot_cast_w(x_ref, w_ref):
    return _dot_chunks(x_ref, slice(None), _w_chunks(w_ref))


def _mm_kernel(*refs, epilogue, n_row, n_tile):
    x_ref, w_ref = refs[0], refs[1]
    rows = [r[...] for r in refs[2:2 + n_row]]
    tiles = [r[...] for r in refs[2 + n_row:2 + n_row + n_tile]]
    acc = _dot_cast_w(x_ref, w_ref)
    res = epilogue(acc, *rows, *tiles)
    for o_ref in refs[2 + n_row + n_tile:]:
        o_ref[...] = res.astype(o_ref.dtype)


def _matmul_tiles(M, K, N, w_bytes, out_bytes, n_tiles):
    budget = VMEM_LIMIT - 6 * 1024 * 1024
    row_tiles = [t for t in (2064, 1376, 688, 512, 256, 128, 64, 16) if M % t == 0] or [M]
    tn_opts = [t for t in (512, 256, 128) if N % t == 0] or [N]

    def fits(tm, tn, x_bufs):
        need = (x_bufs * tm * K * 2 + 2 * K * tn * w_bytes + (W_CAST_CHUNK * tn * 2 if w_bytes > 2 else 0)
                + 2 * tm * tn * (out_bytes + 4 * n_tiles) + tm * tn * 4)
        return need <= budget

    for tn in tn_opts:
        for tm in row_tiles:
            if tm >= min(1024, row_tiles[0]) and fits(tm, tn, 2):
                return tm, tn, 2
        for tm in row_tiles:
            if fits(tm, tn, 1):
                return tm, tn, 1
    return row_tiles[-1], tn_opts[-1], 1


def _mmk_kernel(*refs, epilogue, n_row, n_tile, n_out):
    x_ref, w_ref = refs[0], refs[1]
    row_refs = refs[2:2 + n_row]
    tile_refs = refs[2 + n_row:2 + n_row + n_tile]
    out_refs = refs[2 + n_row + n_tile:2 + n_row + n_tile + n_out]
    acc_ref = refs[-1]
    k = pl.program_id(0)
    part = jnp.dot(x_ref[...], w_ref[...].astype(BF16), preferred_element_type=F32)

    @pl.when(k == 0)
    def _():
        acc_ref[...] = part

    @pl.when(k > 0)
    def _():
        acc_ref[...] += part

    @pl.when(k == pl.num_programs(0) - 1)
    def _():
        res = epilogue(acc_ref[...], *[r[...] for r in row_refs], *[t[...] for t in tile_refs])
        for o_ref in out_refs:
            o_ref[...] = res.astype(o_ref.dtype)


def _matmul_kgrid(x, w, dtypes, epilogue, rows, tiles, widx, N, col_off, name, as_tuple):
    M, K = x.shape
    tk = _pick(K, tuple(t for t in (2048, 1024, 512, 256, 128) if t * N * w.dtype.itemsize <= KGRID_W_BYTES))
    jblk = col_off // N
    full = pl.BlockSpec((M, N), lambda k: (0, 0))
    in_specs = [pl.BlockSpec((M, tk), lambda k: (0, k)),
                pl.BlockSpec((None,) * len(widx) + (tk, N), lambda k: (*widx, k, jblk))]
    in_specs += [pl.BlockSpec((1, N), lambda k: (0, 0)) for _ in rows] + [full for _ in tiles]
    outs = pl.pallas_call(
        functools.partial(_mmk_kernel, epilogue=epilogue, n_row=len(rows), n_tile=len(tiles), n_out=len(dtypes)),
        grid=(K // tk,),
        in_specs=in_specs,
        out_specs=[full for _ in dtypes],
        out_shape=[jax.ShapeDtypeStruct((M, N), d) for d in dtypes],
        scratch_shapes=[pltpu.VMEM((M, N), F32)],
        compiler_params=_cparams(("arbitrary",)),
        name=name,
    )(x, w, *rows, *tiles)
    return outs if as_tuple else outs[0]


def _matmul(x, w, *, out_dtype, epilogue=_epi_none, rows=(), tiles=(), widx=(), n=None, col_off=0, name):
    M, K = x.shape
    N = w.shape[-1] if n is None else n
    dtypes = out_dtype if isinstance(out_dtype, tuple) else (out_dtype,)
    if M <= 512 and w.dtype == F32 and K * N * 4 >= 2 * KGRID_W_BYTES and col_off % N == 0:
        return _matmul_kgrid(x, w, dtypes, epilogue, rows, tiles, widx, N, col_off, name,
                             isinstance(out_dtype, tuple))
    tm, tn, x_bufs = _matmul_tiles(M, K, N, w.dtype.itemsize, sum(jnp.dtype(d).itemsize for d in dtypes),
                                   len(tiles))
    assert col_off % tn == 0 and w.ndim == 2 + len(widx)
    joff = col_off // tn
    x_mode = pl.Buffered(1) if x_bufs == 1 else None
    in_specs = [pl.BlockSpec((tm, K), lambda i, j: (i, 0), pipeline_mode=x_mode),
                pl.BlockSpec((None,) * len(widx) + (K, tn), lambda i, j: (*widx, 0, j + joff))]
    in_specs += [pl.BlockSpec((1, tn), lambda i, j: (0, j)) for _ in rows]
    in_specs += [pl.BlockSpec((tm, tn), lambda i, j: (i, j)) for _ in tiles]
    outs = pl.pallas_call(
        functools.partial(_mm_kernel, epilogue=epilogue, n_row=len(rows), n_tile=len(tiles)),
        grid=(M // tm, N // tn),
        in_specs=in_specs,
        out_specs=[pl.BlockSpec((tm, tn), lambda i, j: (i, j)) for _ in dtypes],
        out_shape=[jax.ShapeDtypeStruct((M, N), d) for d in dtypes],
        compiler_params=_cparams(("parallel", "arbitrary")),
        name=name,
    )(x, w, *rows, *tiles)
    return outs if isinstance(out_dtype, tuple) else outs[0]


def _resnorm_kernel(*refs, n_next):
    h_ref, y_ref, gp_ref = refs[0], refs[1], refs[2]
    gn_refs = refs[3:3 + n_next]
    hn_ref = refs[3 + n_next]
    xo_refs = refs[4 + n_next:]
    hn = h_ref[...] + _rms(y_ref[...], gp_ref[...])
    hn_ref[...] = hn
    for gn_ref, xo_ref in zip(gn_refs, xo_refs):
        xo_ref[...] = _rms(hn, gn_ref[...]).astype(xo_ref.dtype)


def _resnorm(h, y, g_post, g_next):
    M, D = h.shape
    tr = _pick(M, (192, 128, 64, 16))
    n = len(g_next)
    row = pl.BlockSpec((tr, D), lambda i: (i, 0))
    vec = pl.BlockSpec((1, D), lambda i: (0, 0))
    outs = pl.pallas_call(
        functools.partial(_resnorm_kernel, n_next=n),
        grid=(M // tr,),
        in_specs=[row, row, vec] + [vec] * n,
        out_specs=[row] + [row] * n,
        out_shape=[jax.ShapeDtypeStruct((M, D), F32)] + [jax.ShapeDtypeStruct((M, D), BF16)] * n,
        compiler_params=_cparams(("parallel",)),
        name="resnorm",
    )(h, y, g_post.reshape(1, D), *[g.reshape(1, D) for g in g_next])
    return outs[0], list(outs[1:])


def _normmix_kernel(h_ref, halo_ref, g_ref, shift_ref, mix_ref, *out_refs, tt):
    t = pl.program_id(1)
    g = g_ref[...]
    xn = _rms(h_ref[0], g)
    halo = _rms(halo_ref[0], g)
    prev_row = jnp.where(t == 0, shift_ref[0], halo[7:8, :])
    row = lax.broadcasted_iota(jnp.int32, (tt, 1), 0)
    x_prev = jnp.where(row == 0, prev_row, pltpu.roll(xn, 1, axis=0))
    xx = x_prev - xn
    for i in range(6):
        out_refs[i][0] = (xn + xx * mix_ref[i:i + 1, :]).astype(BF16)
    out_refs[6][0] = xn[tt - 1:tt, :]


def _normmix(h, g, shift0, mix):
    B, T, D = h.shape
    tt = _pick(T, (64, 48, 32, 16))
    nb = tt // 8
    tok = pl.BlockSpec((1, tt, D), lambda b, t: (b, t, 0))
    outs = pl.pallas_call(
        functools.partial(_normmix_kernel, tt=tt),
        grid=(B, T // tt),
        in_specs=[tok,
                  pl.BlockSpec((1, 8, D), lambda b, t: (b, jnp.maximum(t * nb - 1, 0), 0)),
                  pl.BlockSpec((1, D), lambda b, t: (0, 0)),
                  pl.BlockSpec((1, 1, D), lambda b, t: (b, 0, 0)),
                  pl.BlockSpec((6, D), lambda b, t: (0, 0))],
        out_specs=[tok] * 6 + [pl.BlockSpec((1, 1, D), lambda b, t: (b, 0, 0))],
        out_shape=[jax.ShapeDtypeStruct((B, T, D), BF16)] * 6 + [jax.ShapeDtypeStruct((B, 1, D), F32)],
        compiler_params=_cparams(("parallel", "arbitrary")),
        name="normmix",
    )(h, h, g.reshape(1, D), shift0.reshape(B, 1, D), mix)
    return list(outs[:6]), outs[6].reshape(B, D)


def _wkv_phase1(c0, n_valid, lanes, refs, prm, bones):
    r_ref, k_ref, v_ref, ld_ref, a_ref = refs
    kkw, kaw, rkw = prm
    G, LW, C = WKV_GROUP, WKV_LANES, WKV_CHUNK
    R = G * C
    sl = pl.ds(c0, C)
    r = r_ref[0, sl, lanes]
    k = k_ref[0, sl, lanes]
    v = v_ref[0, sl, lanes]
    ld = ld_ref[0, sl, lanes]
    a = a_ref[0, sl, lanes]
    if n_valid < C:
        valid = lax.broadcasted_iota(jnp.int32, (C, 1), 0) < n_valid
        r, k, v, ld = (jnp.where(valid, x, 0.0) for x in (r, k, v, ld))

    kk = k * kkw
    km = k * (1.0 + (a - 1.0) * kaw)
    ti = lax.broadcasted_iota(jnp.int32, (C, C), 0)
    si = lax.broadcasted_iota(jnp.int32, (C, C), 1)
    sums = _split_rows_dot([kk * kk, r * km * rkw], bones, 2)
    kk_ss, rk_sum = sums[0], sums[1]
    tri = jnp.where(ti >= si, 1.0, 0.0).astype(BF16)
    cum3 = jnp.dot(tri, jnp.concatenate(_split(ld, 3), axis=1), preferred_element_type=F32)
    cum = cum3[:, :LW] + cum3[:, LW:2 * LW] + cum3[:, 2 * LW:]
    yield
    kk = kk / jnp.maximum(jnp.sqrt(kk_ss), 1e-12)
    bonus = rk_sum * v
    b = kk * a
    cum_last = cum[C - 1:C, :]
    e_neg = jnp.exp(-cum)
    e_rem = jnp.exp(cum_last - cum)
    lane_head = lax.broadcasted_iota(jnp.int32, (1, LW), 1) // HEAD_A

    def stack(x):
        return jnp.concatenate([jnp.where(lane_head == h, x, 0.0) for h in range(G)], axis=0)

    r_s = stack(jnp.exp(cum) * r)
    a_b = stack(jnp.exp(cum - ld) * (-kk)).astype(BF16)
    v_b = stack(v).astype(BF16)
    ar_b = jnp.concatenate([a_b, r_s.astype(BF16)], axis=0)
    bk_b = jnp.concatenate([stack(e_neg * b), stack(e_neg * km)], axis=0).astype(BF16)
    bkh_t = jnp.concatenate([stack(e_rem * b).T, stack(e_rem * km).T], axis=1).astype(BF16)

    big = _bdot_nt(ar_b, bk_b)
    yield
    row = lax.broadcasted_iota(jnp.int32, (R, R), 0)
    col = lax.broadcasted_iota(jnp.int32, (R, R), 1)
    same = (row // C) == (col // C)
    strict = same & (row > col)
    incl = same & (row >= col)
    n_ab = jnp.where(strict, big[:R, :R], 0.0)
    a_ak = jnp.where(strict, big[:R, R:], 0.0)
    a_rb = jnp.where(incl, big[R:, :R], 0.0).astype(BF16)
    a_rk = jnp.where(incl, big[R:, R:], 0.0)
    av = _bdot(jnp.concatenate([a_ak, a_rk], axis=0), v_b)

    tinv = jnp.where(row == col, 1.0, 0.0) + n_ab
    npow = n_ab
    span = 2
    while span < C:
        npow_b = npow.astype(BF16)
        if span > 2:
            both = _bdot(jnp.concatenate([npow_b, tinv.astype(BF16)], axis=0), npow_b)
            npow, tinv = both[:R], tinv + both[R:]
        else:
            npow = _bdot(npow_b, npow_b)
        span *= 2
        yield
    tinv = tinv + _bdot(tinv, npow)
    yield

    wu_b = _bdot(tinv, jnp.concatenate([a_b, av[:R].astype(BF16)], axis=1)).astype(BF16)
    yield
    qo = _bdot(a_rb, wu_b)
    low = jnp.concatenate([jnp.zeros((R, LW), BF16), v_b], axis=1)
    m = _bdot(bkh_t, jnp.concatenate([wu_b, low], axis=0))
    yield
    q_b = (r_s + qo[:, :LW]).astype(BF16)
    o0 = qo[:, LW:] + av[R:]
    li = lax.broadcasted_iota(jnp.int32, (LW, LW), 0)
    lj = lax.broadcasted_iota(jnp.int32, (LW, LW), 1)
    m1_b = (jnp.where(li == lj, jnp.exp(cum_last), 0.0) + m[:, :LW]).astype(BF16)
    return jnp.concatenate([q_b, m1_b], axis=0), o0, m[:, LW:], bonus


def _wkv_phase2(state, ph, c0, n_valid, lanes, g_ref, y_ref, lnw, lnb, bones):
    qm_b, o0, m2, bonus = ph
    C = WKV_CHUNK
    R = WKV_GROUP * C
    qs = _bdot(qm_b, state)
    yield
    o_st = qs[:R] + o0
    new_state = qs[R:] + m2
    o = o_st[0:C, :]
    for h in range(1, WKV_GROUP):
        o = o + o_st[h * C:(h + 1) * C, :]
    mu = _split_rows_dot([o], bones, 2)[0] * (1.0 / HEAD_A)
    yield
    d = o - mu
    var = _split_rows_dot([d * d], bones, 2)[0] * (1.0 / HEAD_A)
    yield
    o = d * lax.rsqrt(var + GN_EPS) * lnw + lnb + bonus
    sl = pl.ds(c0, n_valid)
    y_ref[0, sl, lanes] = (o[:n_valid] * g_ref[0, sl, lanes]).astype(y_ref.dtype)
    return new_state


def _wkv_kernel(r_ref, k_ref, v_ref, ld_ref, a_ref, g_ref, kkw_ref, kaw_ref, rkw_ref, lnw_ref, lnb_ref,
                s0_ref, y_ref, sout_ref, *, T, n_tiles, n_par):
    G, LW, N = WKV_GROUP, WKV_LANES, HEAD_A
    li = lax.broadcasted_iota(jnp.int32, (LW, LW), 0)
    lj = lax.broadcasted_iota(jnp.int32, (LW, LW), 1)
    bones = jnp.where((li // N) == (lj // N), 1.0, 0.0).astype(BF16)
    refs = (r_ref, k_ref, v_ref, ld_ref, a_ref)
    lane_sl = [slice(p * LW, (p + 1) * LW) for p in range(n_tiles)]
    prm = [(kkw_ref[:, s], kaw_ref[:, s], rkw_ref[:, s]) for s in lane_sl]
    post = [(lnw_ref[:, s], lnb_ref[:, s]) for s in lane_sl]

    def steps(c_base, n_valid, n_chunks, states):
        C = WKV_CHUNK
        ph = _interleave([_wkv_phase1(c_base + u * C, n_valid, lane_sl[p], refs, prm[p], bones)
                          for u in range(n_chunks) for p in range(n_tiles)])
        states = list(states)
        for u in range(n_chunks):
            states = _interleave([
                _wkv_phase2(states[p].astype(BF16), ph[u * n_tiles + p], c_base + u * C, n_valid, lane_sl[p],
                            g_ref, y_ref, post[p][0], post[p][1], bones) for p in range(n_tiles)])
        return tuple(states)

    zero = jnp.zeros((N, N), F32)
    states = []
    for p in range(n_tiles):
        blocks = [jnp.concatenate([s0_ref[0, p * G + h].T if j == h else zero for j in range(G)], axis=1)
                  for h in range(G)]
        states.append(jnp.concatenate(blocks, axis=0))
    states = tuple(states)

    head = T % WKV_CHUNK
    if head:
        states = steps(0, head, 1, states)
    n_main = T // WKV_CHUNK
    if n_main:
        def body(i, st):
            return steps(pl.multiple_of(head + i * (n_par * WKV_CHUNK), 16), WKV_CHUNK, n_par, st)

        states = lax.fori_loop(0, n_main // n_par, body, states)

    for p in range(n_tiles):
        for h in range(G):
            sout_ref[0, p * G + h] = states[p][h * N:(h + 1) * N, h * N:(h + 1) * N].T


def _wkv(r, k, v, ld, a, g, kkw, kaw, rkw, lnw, lnb, s0):
    B, T, D = r.shape
    LW, G = WKV_LANES, WKV_GROUP
    n_main = T // WKV_CHUNK
    assert T >= WKV_CHUNK
    n_par = _pick(n_main, (4, 2, 1))
    n_tiles = _pick(D // LW, (2,)) if n_main > 1 else _pick(D // LW, (8, 4, 2))
    bw = n_tiles * LW
    tok = pl.BlockSpec((1, T, bw), lambda b, j: (b, 0, j))
    vec = pl.BlockSpec((1, bw), lambda b, j: (0, j))
    st = pl.BlockSpec((1, n_tiles * G, HEAD_A, HEAD_A), lambda b, j: (b, j, 0, 0))
    y, s_out = pl.pallas_call(
        functools.partial(_wkv_kernel, T=T, n_tiles=n_tiles, n_par=n_par),
        grid=(B, D // bw),
        in_specs=[tok] * 6 + [vec] * 5 + [st],
        out_specs=[tok, st],
        out_shape=[jax.ShapeDtypeStruct((B, T, D), BF16), jax.ShapeDtypeStruct(s0.shape, F32)],
        compiler_params=_cparams(("parallel", "parallel")),
        name="wkv7",
    )(r, k, v, ld, a, g, *[p.reshape(1, D) for p in (kkw, kaw, rkw, lnw, lnb)], s0)
    return y, s_out


def _sb_consts(nk):
    ji = lax.broadcasted_iota(jnp.int32, (nk, nk), 0)
    si = lax.broadcasted_iota(jnp.int32, (nk, nk), 1)
    later = jnp.where(ji > si, -1.0, 0.0).astype(BF16)
    return jnp.concatenate([later, later], axis=0)


def _sb_pair(q, kt, vt, carry, acc, later2, read):
    nq, nk = q.shape[0], kt.shape[0]
    z = _bdot_nt(q, kt) * (HEAD_B ** -0.5)
    yield
    sp = jnp.maximum(z, 0.0) + jnp.log(1.0 + jnp.exp2(jnp.abs(z) * (-LOG2E)))
    spm = sp if read is None else jnp.where(read, sp, 0.0)
    hi, lo = _split(spm, 2)
    if nk % V7X_LANES == 0:
        local = jnp.dot(jnp.concatenate([hi, lo], axis=1), later2, preferred_element_type=F32)
    else:
        local = (jnp.dot(hi, later2[:nk], preferred_element_type=F32)
                 + jnp.dot(lo, later2[:nk], preferred_element_type=F32))
    yield
    total = jnp.broadcast_to(local[:, 0:1] - spm[:, 0:1], (nq, V7X_LANES))
    if nk % V7X_LANES == 0:
        after = local + jnp.concatenate([carry] * (nk // V7X_LANES), axis=1)
    else:
        after = local + carry[:, :nk]
    w = jnp.exp((z - sp) + after)
    if read is not None:
        w = jnp.where(read, w, 0.0)
    return carry + total, acc + _bdot(w, vt)


def _attn_kernel(*refs, T, P, KN, KP):
    if P:
        q_ref, kn_ref, vn_ref, kp_ref, vp_ref, o_ref = refs
    else:
        q_ref, kn_ref, vn_ref, o_ref = refs
    TQ, HB = ATT_TQ, HEAD_B
    n_heads = q_ref.shape[2] // HB
    lanes = [slice(h * HB, (h + 1) * HB) for h in range(n_heads)]
    n_full, rem = T // TQ, T % TQ
    c_new = _sb_consts(KN)
    c_past = _sb_consts(KP) if P else None

    def sweep(qs, ca, n_new):
        def step(k_ref, v_ref, width, consts, jj, ca):
            rows = pl.ds(pl.multiple_of(jj * width, width), width)
            return tuple(_interleave([_sb_pair(qs[h], k_ref[0, rows, lanes[h]], v_ref[0, rows, lanes[h]],
                                               ca[h][0], ca[h][1], consts, None) for h in range(n_heads)]))

        ca = lax.fori_loop(0, n_new, lambda j, c: step(kn_ref, vn_ref, KN, c_new, n_new - 1 - j, c), ca)
        if P:
            n_past = P // KP
            ca = lax.fori_loop(0, n_past, lambda j, c: step(kp_ref, vp_ref, KP, c_past, n_past - 1 - j, c), ca)
        return ca

    def query_tile(q0, nq, k0, nk, off, consts):
        qrows = pl.ds(q0, nq)
        krows = pl.ds(k0, nk)
        qs = [q_ref[0, qrows, lanes[h]] for h in range(n_heads)]
        cmr = lax.broadcasted_iota(jnp.int32, (nq, nk), 1) - lax.broadcasted_iota(jnp.int32, (nq, nk), 0)
        read = cmr < off
        zc = jnp.zeros((nq, V7X_LANES), F32)
        za = jnp.zeros((nq, HB), F32)
        ca = tuple(_interleave([_sb_pair(qs[h], kn_ref[0, krows, lanes[h]], vn_ref[0, krows, lanes[h]],
                                         zc, za, consts, read) for h in range(n_heads)]))
        ca = sweep(qs, ca, k0 // KN)
        for h in range(n_heads):
            o_ref[0, qrows, lanes[h]] = ca[h][1].astype(o_ref.dtype)

    if n_full:
        def full_tile(i, c):
            q0 = pl.multiple_of(i * TQ, TQ)
            k0 = pl.multiple_of((q0 // KN) * KN, KN)
            query_tile(q0, TQ, k0, KN, q0 - k0, c_new)
            return c

        lax.fori_loop(0, n_full, full_tile, 0)
    if rem:
        q0 = n_full * TQ
        query_tile(q0, rem, q0, rem, 0, _sb_consts(rem))


def _attention(q, k_new, v_new, k_past, v_past):
    B, T, D = q.shape
    P = 0 if k_past is None else k_past.shape[1]
    full = T - T % ATT_TQ
    KN = V7X_MXU_DIM if full % V7X_MXU_DIM == 0 and full else ATT_TQ
    KP = V7X_MXU_DIM if P % V7X_MXU_DIM == 0 else ATT_TQ
    assert P % KP == 0 and full % KN == 0
    bw = HEAD_B * _pick(D // HEAD_B, (ATT_HEADS, 2, 1))
    new = pl.BlockSpec((1, T, bw), lambda b, h: (b, 0, h))
    in_specs = [new, new, new]
    args = [q, k_new, v_new]
    if P:
        past = pl.BlockSpec((1, P, bw), lambda b, h: (b, 0, h))
        in_specs += [past, past]
        args += [k_past, v_past]
    return pl.pallas_call(
        functools.partial(_attn_kernel, T=T, P=P, KN=KN, KP=KP),
        grid=(B, D // bw),
        in_specs=in_specs,
        out_specs=new,
        out_shape=jax.ShapeDtypeStruct((B, T, D), BF16),
        compiler_params=_cparams(("parallel", "parallel")),
        name="sb_attention",
    )(*args)


def _ffn_up_kernel(x_ref, wg_ref, wu_ref, st_ref, cw_ref, cb_ref, act_ref, nst_ref, *, T, spt, subs):
    tm, tn = act_ref.shape
    wg = _w_chunks(wg_ref)
    wu = _w_chunks(wu_ref)
    cw = cw_ref[...]
    cb = cb_ref[...]
    st = st_ref[...]
    if spt == 1:
        s0, s1 = st[0, 0:1, :], st[0, 1:2, :]
    else:
        s0 = jnp.broadcast_to(st[:, 0:1, :], (spt, T, tn)).reshape(tm, tn)
        s1 = jnp.broadcast_to(st[:, 1:2, :], (spt, T, tn)).reshape(tm, tn)
    prev = None
    r0 = 0
    for rs in subs:
        rows = slice(r0, r0 + rs)
        lrow = lax.broadcasted_iota(jnp.int32, (rs, 1), 0)
        gate = _dot_chunks(x_ref, rows, wg)
        up = _dot_chunks(x_ref, rows, wu)
        g1 = pltpu.roll(gate, 1, axis=0)
        g2 = pltpu.roll(gate, 2, axis=0)
        if prev is not None:
            g1 = jnp.where(lrow == 0, prev[1:2, :], g1)
            g2 = jnp.where(lrow == 0, prev[0:1, :], jnp.where(lrow == 1, prev[1:2, :], g2))
        pos = (lrow + r0) % T
        g1 = jnp.where(pos == 0, s1, g1)
        g2 = jnp.where(pos == 0, s0, jnp.where(pos == 1, s1, g2))
        conv = g2 * cw[0:1, :] + g1 * cw[1:2, :] + gate * cw[2:3, :] + cb
        act_ref[rows, :] = (jax.nn.silu(conv) * up).astype(act_ref.dtype)
        prev = gate[rs - 2:rs, :]
        r0 += rs
        if spt > 1:
            nst_ref[...] = gate.reshape(spt, T, tn)[:, T - 2:T, :]
    if spt == 1:
        nst_ref[0] = prev


def _ffn_up(x, w_up, layer, conv_state, conv_w, conv_b, T):
    M, D = x.shape
    F = w_up.shape[-1] // 2
    n_seq = M // T
    spt = _pick(n_seq, (8, 4, 2, 1)) if T * 8 <= 512 else 1
    tm = spt * T
    subs = [tm]
    if spt == 1 and tm >= 512 and tm % 16 == 0:
        subs, rem = [], tm
        while rem > 640:
            subs.append(512)
            rem -= 512
        subs += [rem - 128, 128] if rem > 256 else [rem]
    tn = _pick(F, (256, 128))
    nf = F // tn
    x_mode = pl.Buffered(1) if tm * D * 2 > 8 * 1024 * 1024 else None
    st_spec = pl.BlockSpec((spt, 2, tn), lambda i, j: (i, 0, j))
    return pl.pallas_call(
        functools.partial(_ffn_up_kernel, T=T, spt=spt, subs=tuple(subs)),
        grid=(M // tm, nf),
        in_specs=[pl.BlockSpec((tm, D), lambda i, j: (i, 0), pipeline_mode=x_mode),
                  pl.BlockSpec((None, D, tn), lambda i, j: (layer, 0, j)),
                  pl.BlockSpec((None, D, tn), lambda i, j: (layer, 0, j + nf)),
                  st_spec,
                  pl.BlockSpec((CONV_W, tn), lambda i, j: (0, j)),
                  pl.BlockSpec((1, tn), lambda i, j: (0, j))],
        out_specs=[pl.BlockSpec((tm, tn), lambda i, j: (i, j)), st_spec],
        out_shape=[jax.ShapeDtypeStruct((M, F), BF16), jax.ShapeDtypeStruct((n_seq, 2, F), F32)],
        compiler_params=_cparams(("parallel", "arbitrary")),
        name="ffn_up_conv",
    )(x, w_up, w_up, conv_state, conv_w, conv_b.reshape(1, F))


def _run_group(h, shift0, wkv0, conv0, past_k, past_v, p, depth, n_a):
    B, T, D = h.shape
    M = B * T
    flat = lambda t: t.reshape(M, D)
    seq = lambda t: t.reshape(B, T, D)
    gains = p['norm_gains']
    new_wkv, new_shift, new_conv = [], [], []
    v_first = None
    k_new = v_new = k_bf = v_bf = None
    h = flat(h)
    xin = None
    xkv = None
    for layer in range(depth):
        g = gains[layer]
        if layer < n_a:
            i = layer
            mixed_in, last_row = _normmix(seq(h), g[0], shift0[i], p['a_mix'][i])
            xr, xw, xk, xv, xa, xg = (flat(t) for t in mixed_in)
            new_shift.append(last_row)
            r = _matmul(xr, p['a_w_rkv'], widx=(i, 0), out_dtype=F32, name="rwkv_r")
            k = _matmul(xk, p['a_w_rkv'], widx=(i, 1), out_dtype=F32, name="rwkv_k")
            v = _matmul(xv, p['a_w_rkv'], widx=(i, 2), out_dtype=F32, name="rwkv_v")
            ld = _matmul(_matmul(xw, p['a_w1'], widx=(i,), out_dtype=BF16, epilogue=_epi_tanh, name="lora_w1"),
                         p['a_w2'], widx=(i,), out_dtype=F32, epilogue=_epi_log_decay,
                         rows=(p['a_w0'][i].reshape(1, D),), name="lora_w2")
            a = _matmul(_matmul(xa, p['a_a1'], widx=(i,), out_dtype=BF16, name="lora_a1"),
                        p['a_a2'], widx=(i,), out_dtype=F32, epilogue=_epi_bias_sigmoid,
                        rows=(p['a_a0'][i].reshape(1, D),), name="lora_a2")
            gate = _matmul(_matmul(xg, p['a_g1'], widx=(i,), out_dtype=BF16, epilogue=_epi_sigmoid,
                                   name="lora_g1"),
                           p['a_g2'], widx=(i,), out_dtype=F32, name="lora_g2")
            if i == 0:
                v_first = v
            else:
                v = _matmul(_matmul(xv, p['a_v1'], widx=(i - 1,), out_dtype=BF16, name="lora_v1"),
                            p['a_v2'], widx=(i - 1,), out_dtype=F32, epilogue=_epi_value_mix,
                            rows=(p['a_v0'][i - 1].reshape(1, D),), tiles=(v, v_first), name="lora_v2")
            y, s_out = _wkv(seq(r), seq(k), seq(v), seq(ld), seq(a), seq(gate),
                            p['a_k_k'][i], p['a_k_a'][i], p['a_r_k'][i].reshape(D), p['a_ln_w'][i],
                            p['a_ln_b'][i], wkv0[i])
            new_wkv.append(s_out)
            mixed = _matmul(flat(y), p['a_w_out'], widx=(i,), out_dtype=F32, name="rwkv_out")
        else:
            j = layer - n_a
            if layer == n_a:
                k_new, k_bf = _matmul(xkv, p['w_kv'], out_dtype=(F32, BF16), n=D, name="kv_proj_k")
                v_new, v_bf = _matmul(xkv, p['w_kv'], out_dtype=(F32, BF16), n=D, col_off=D, name="kv_proj_v")
            q = _matmul(xin, p['b_w_q'], widx=(j,), out_dtype=BF16, name="attn_q")
            attn = _attention(seq(q), seq(k_bf), seq(v_bf), past_k, past_v)
            mixed = _matmul(flat(attn), p['b_w_out'], widx=(j,), out_dtype=F32, name="attn_out")
        h, (xf,) = _resnorm(h, mixed, g[1], [g[2]])
        act, conv_new = _ffn_up(xf, p['ffn_w_up'], layer, conv0[layer], p['ffn_conv_w'][layer],
                                p['ffn_conv_b'][layer], T)
        new_conv.append(conv_new)
        f = _matmul(act, p['ffn_w_down'], widx=(layer,), out_dtype=F32, name="ffn_down")
        nxt = []
        if layer + 1 < depth and layer + 1 >= n_a:
            nxt.append(gains[layer + 1][0])
            if layer + 1 == n_a:
                nxt.append(p['kv_norm'])
        h, normed = _resnorm(h, f, g[3], nxt)
        if normed:
            xin = normed[0]
            if len(normed) > 1:
                xkv = normed[1]
    return (seq(h), seq(k_new), seq(v_new), jnp.stack(new_wkv), jnp.stack(new_shift),
            jnp.stack(new_conv))


def kernel(x_prompt, x_sample, cache_k, cache_v, state_wkv, state_shift, state_ffn_conv, meta_tokens, norm_gains, a_mix, a_w_rkv, a_w_out, a_w0, a_w1, a_w2, a_a0, a_a1, a_a2, a_v0, a_v1, a_v2, a_g1, a_g2, a_k_k, a_k_a, a_r_k, a_ln_w, a_ln_b, kv_norm, w_kv, b_w_q, b_w_out, ffn_w_up, ffn_conv_w, ffn_conv_b, ffn_w_down):
    B, S, D = x_prompt.shape
    Bs, Ts, _ = x_sample.shape
    depth = norm_gains.shape[0]
    n_a = a_mix.shape[0]
    F = ffn_w_down.shape[1]
    H_a, H_b = D // HEAD_A, D // HEAD_B
    assert n_a >= 1 and depth > n_a
    bf = lambda w: w.astype(BF16)
    p = {
        'norm_gains': norm_gains, 'a_mix': a_mix, 'a_w_rkv': a_w_rkv, 'a_w_out': a_w_out,
        'a_w0': a_w0, 'a_w1': a_w1, 'a_w2': a_w2, 'a_a0': a_a0, 'a_a1': a_a1,
        'a_a2': a_a2, 'a_v0': a_v0, 'a_v1': a_v1, 'a_v2': a_v2, 'a_g1': a_g1,
        'a_g2': a_g2, 'a_k_k': a_k_k, 'a_k_a': a_k_a, 'a_r_k': a_r_k, 'a_ln_w': a_ln_w,
        'a_ln_b': a_ln_b, 'kv_norm': kv_norm, 'w_kv': w_kv, 'b_w_q': b_w_q,
        'b_w_out': b_w_out, 'ffn_w_up': ffn_w_up, 'ffn_conv_w': ffn_conv_w,
        'ffn_conv_b': ffn_conv_b, 'ffn_w_down': bf(ffn_w_down),
    }
    dt = x_prompt.dtype
    T = N_META + S
    meta = jnp.broadcast_to(meta_tokens[None].astype(dt), (B, N_META, D))
    h0 = jnp.concatenate([meta, x_prompt], axis=1)
    h_p, k_p, v_p, wkv_p, shift_p, conv_p = _run_group(
        h0, jnp.zeros((n_a, B, D), dt), jnp.zeros((n_a, B, H_a, HEAD_A, HEAD_A), dt),
        jnp.zeros((depth, B, CONV_W - 1, F), dt), None, None, p, depth, n_a)
    P = cache_k.shape[1]
    h_s, k_s, v_s, wkv_s, shift_s, conv_s = _run_group(
        x_sample, state_shift, state_wkv, state_ffn_conv, bf(cache_k).reshape(Bs, P, D),
        bf(cache_v).reshape(Bs, P, D), p, depth, n_a)
    heads = lambda t: t.reshape(t.shape[0], t.shape[1], H_b, HEAD_B)
    return (h_p[:, N_META:], h_s, heads(k_p), heads(v_p), heads(k_s), heads(v_s),
            wkv_p, wkv_s, shift_p, shift_s, conv_p, conv_s)
```

```python
import functools

import jax
import jax.numpy as jnp
from jax import lax
from jax.experimental import pallas as pl
from jax.experimental.pallas import tpu as pltpu

F32 = jnp.float32
BF16 = jnp.bfloat16

N_META = 16
HEAD_A = 64
HEAD_B = 128
GN_EPS = 64e-5
RMS_EPS = 1e-6
CONV_W = 3

V7X_LANES = 128
V7X_MXU_DIM = 256
V7X_VMEM_BYTES = 64 * 1024 * 1024
VMEM_LIMIT = V7X_VMEM_BYTES - 8 * 1024 * 1024

W_CAST_CHUNK = 1024
KGRID_W_BYTES = 8 * 1024 * 1024
LOG2E = 1.4426950408889634
EXP_NEG_HALF = -0.6065306597126334

WKV_GROUP = 2
WKV_LANES = WKV_GROUP * HEAD_A
WKV_CHUNK = 64
ATT_TQ = 256
ATT_HEADS = 4


def _pick(n, prefs):
    for p in prefs:
        if p <= n and n % p == 0:
            return p
    return n


def _cparams(sem):
    return pltpu.CompilerParams(dimension_semantics=sem, vmem_limit_bytes=VMEM_LIMIT)


def _bdot(a, b):
    return jnp.dot(a.astype(BF16), b.astype(BF16), preferred_element_type=F32)


def _bdot_nt(a, b):
    return lax.dot_general(a.astype(BF16), b.astype(BF16), (((1,), (1,)), ((), ())),
                           preferred_element_type=F32)


def _bdot_tn(a, b):
    return lax.dot_general(a.astype(BF16), b.astype(BF16), (((0,), (0,)), ((), ())),
                           preferred_element_type=F32)


def _split(x, parts):
    out = []
    rem = x
    for i in range(parts):
        hi = rem.astype(BF16)
        out.append(hi)
        if i + 1 < parts:
            rem = rem - hi.astype(F32)
    return out


def _split_dot(x, m_bf16, parts):
    acc = None
    for hi in _split(x, parts):
        t = jnp.dot(hi, m_bf16, preferred_element_type=F32)
        acc = t if acc is None else acc + t
    return acc


def _split_rows_dot(xs, m_bf16, parts):
    n = xs[0].shape[0]
    stacked = jnp.concatenate([p for x in xs for p in _split(x, parts)], axis=0)
    res = jnp.dot(stacked, m_bf16, preferred_element_type=F32)
    out = []
    for i in range(len(xs)):
        acc = res[i * parts * n:(i * parts + 1) * n]
        for j in range(1, parts):
            acc = acc + res[(i * parts + j) * n:(i * parts + j + 1) * n]
        out.append(acc)
    return out


def _split_dot_lhs(m_bf16, x, parts):
    acc = None
    for hi in _split(x, parts):
        t = jnp.dot(m_bf16, hi, preferred_element_type=F32)
        acc = t if acc is None else acc + t
    return acc


def _interleave(gens):
    results = [None] * len(gens)
    live = list(range(len(gens)))
    while live:
        still = []
        for i in live:
            try:
                next(gens[i])
                still.append(i)
            except StopIteration as stop:
                results[i] = stop.value
        live = still
    return results


def _rms(x, g):
    return x * lax.rsqrt(jnp.mean(x * x, axis=-1, keepdims=True) + RMS_EPS) * g


def _epi_none(acc):
    return acc


def _epi_tanh(acc):
    return jnp.tanh(acc)


def _epi_sigmoid(acc):
    return jax.nn.sigmoid(acc)


def _epi_bias_sigmoid(acc, bias):
    return jax.nn.sigmoid(bias + acc)


def _epi_log_decay(acc, w0):
    return EXP_NEG_HALF * jax.nn.sigmoid(w0 + acc)


def _epi_value_mix(acc, v0, v, v_first):
    return v + (v_first - v) * jax.nn.sigmoid(v0 + acc)


def _w_chunks(w_ref):
    K = w_ref.shape[0]
    if w_ref.dtype == BF16 or K % W_CAST_CHUNK or K == W_CAST_CHUNK:
        return [(slice(None), w_ref[...].astype(BF16))]
    return [(slice(k0, k0 + W_CAST_CHUNK), w_ref[k0:k0 + W_CAST_CHUNK, :].astype(BF16))
            for k0 in range(0, K, W_CAST_CHUNK)]


def _dot_chunks(x_ref, rows, chunks):
    acc = None
    for ks, w in chunks:
        t = jnp.dot(x_ref[rows, ks], w, preferred_element_type=F32)
        acc = t if acc is None else acc + t
    return acc


def _dot_cast_w(x_ref, w_ref):
    return _dot_chunks(x_ref, slice(None), _w_chunks(w_ref))


def _mm_kernel(*refs, epilogue, n_row, n_tile):
    x_ref, w_ref = refs[0], refs[1]
    rows = [r[...] for r in refs[2:2 + n_row]]
    tiles = [r[...] for r in refs[2 + n_row:2 + n_row + n_tile]]
    acc = _dot_cast_w(x_ref, w_ref)
    res = epilogue(acc, *rows, *tiles)
    for o_ref in refs[2 + n_row + n_tile:]:
        o_ref[...] = res.astype(o_ref.dtype)


def _matmul_tiles(M, K, N, w_bytes, out_bytes, n_tiles):
    budget = VMEM_LIMIT - 6 * 1024 * 1024
    row_tiles = [t for t in (2064, 1376, 688, 512, 256, 128, 64, 16) if M % t == 0] or [M]
    tn_opts = [t for t in (512, 256, 128) if N % t == 0] or [N]

    def fits(tm, tn, x_bufs):
        need = (x_bufs * tm * K * 2 + 2 * K * tn * w_bytes + (W_CAST_CHUNK * tn * 2 if w_bytes > 2 else 0)
                + 2 * tm * tn * (out_bytes + 4 * n_tiles) + tm * tn * 4)
        return need <= budget

    for tn in tn_opts:
        for tm in row_tiles:
            if tm >= min(1024, row_tiles[0]) and fits(tm, tn, 2):
                return tm, tn, 2
        for tm in row_tiles:
            if fits(tm, tn, 1):
                return tm, tn, 1
    return row_tiles[-1], tn_opts[-1], 1


def _mmk_kernel(*refs, epilogue, n_row, n_tile, n_out):
    x_ref, w_ref = refs[0], refs[1]
    row_refs = refs[2:2 + n_row]
    tile_refs = refs[2 + n_row:2 + n_row + n_tile]
    out_refs = refs[2 + n_row + n_tile:2 + n_row + n_tile + n_out]
    acc_ref = refs[-1]
    k = pl.program_id(0)
    part = jnp.dot(x_ref[...], w_ref[...].astype(BF16), preferred_element_type=F32)

    @pl.when(k == 0)
    def _():
        acc_ref[...] = part

    @pl.when(k > 0)
    def _():
        acc_ref[...] += part

    @pl.when(k == pl.num_programs(0) - 1)
    def _():
        res = epilogue(acc_ref[...], *[r[...] for r in row_refs], *[t[...] for t in tile_refs])
        for o_ref in out_refs:
            o_ref[...] = res.astype(o_ref.dtype)


def _matmul_kgrid(x, w, dtypes, epilogue, rows, tiles, widx, N, col_off, name, as_tuple):
    M, K = x.shape
    tk = _pick(K, tuple(t for t in (2048, 1024, 512, 256, 128) if t * N * w.dtype.itemsize <= KGRID_W_BYTES))
    jblk = col_off // N
    full = pl.BlockSpec((M, N), lambda k: (0, 0))
    in_specs = [pl.BlockSpec((M, tk), lambda k: (0, k)),
                pl.BlockSpec((None,) * len(widx) + (tk, N), lambda k: (*widx, k, jblk))]
    in_specs += [pl.BlockSpec((1, N), lambda k: (0, 0)) for _ in rows] + [full for _ in tiles]
    outs = pl.pallas_call(
        functools.partial(_mmk_kernel, epilogue=epilogue, n_row=len(rows), n_tile=len(tiles), n_out=len(dtypes)),
        grid=(K // tk,),
        in_specs=in_specs,
        out_specs=[full for _ in dtypes],
        out_shape=[jax.ShapeDtypeStruct((M, N), d) for d in dtypes],
        scratch_shapes=[pltpu.VMEM((M, N), F32)],
        compiler_params=_cparams(("arbitrary",)),
        name=name,
    )(x, w, *rows, *tiles)
    return outs if as_tuple else outs[0]


def _matmul(x, w, *, out_dtype, epilogue=_epi_none, rows=(), tiles=(), widx=(), n=None, col_off=0, name):
    M, K = x.shape
    N = w.shape[-1] if n is None else n
    dtypes = out_dtype if isinstance(out_dtype, tuple) else (out_dtype,)
    if M <= 512 and w.dtype == F32 and K * N * 4 >= 2 * KGRID_W_BYTES and col_off % N == 0:
        return _matmul_kgrid(x, w, dtypes, epilogue, rows, tiles, widx, N, col_off, name,
                             isinstance(out_dtype, tuple))
    tm, tn, x_bufs = _matmul_tiles(M, K, N, w.dtype.itemsize, sum(jnp.dtype(d).itemsize for d in dtypes),
                                   len(tiles))
    assert col_off % tn == 0 and w.ndim == 2 + len(widx)
    joff = col_off // tn
    x_mode = pl.Buffered(1) if x_bufs == 1 else None
    in_specs = [pl.BlockSpec((tm, K), lambda i, j: (i, 0), pipeline_mode=x_mode),
                pl.BlockSpec((None,) * len(widx) + (K, tn), lambda i, j: (*widx, 0, j + joff))]
    in_specs += [pl.BlockSpec((1, tn), lambda i, j: (0, j)) for _ in rows]
    in_specs += [pl.BlockSpec((tm, tn), lambda i, j: (i, j)) for _ in tiles]
    outs = pl.pallas_call(
        functools.partial(_mm_kernel, epilogue=epilogue, n_row=len(rows), n_tile=len(tiles)),
        grid=(M // tm, N // tn),
        in_specs=in_specs,
        out_specs=[pl.BlockSpec((tm, tn), lambda i, j: (i, j)) for _ in dtypes],
        out_shape=[jax.ShapeDtypeStruct((M, N), d) for d in dtypes],
        compiler_params=_cparams(("parallel", "arbitrary")),
        name=name,
    )(x, w, *rows, *tiles)
    return outs if isinstance(out_dtype, tuple) else outs[0]


def _resnorm_kernel(*refs, n_next):
    h_ref, y_ref, gp_ref = refs[0], refs[1], refs[2]
    gn_refs = refs[3:3 + n_next]
    hn_ref = refs[3 + n_next]
    xo_refs = refs[4 + n_next:]
    hn = h_ref[...] + _rms(y_ref[...], gp_ref[...])
    hn_ref[...] = hn
    for gn_ref, xo_ref in zip(gn_refs, xo_refs):
        xo_ref[...] = _rms(hn, gn_ref[...]).astype(xo_ref.dtype)


def _resnorm(h, y, g_post, g_next):
    M, D = h.shape
    tr = _pick(M, (192, 128, 64, 16))
    n = len(g_next)
    row = pl.BlockSpec((tr, D), lambda i: (i, 0))
    vec = pl.BlockSpec((1, D), lambda i: (0, 0))
    outs = pl.pallas_call(
        functools.partial(_resnorm_kernel, n_next=n),
        grid=(M // tr,),
        in_specs=[row, row, vec] + [vec] * n,
        out_specs=[row] + [row] * n,
        out_shape=[jax.ShapeDtypeStruct((M, D), F32)] + [jax.ShapeDtypeStruct((M, D), BF16)] * n,
        compiler_params=_cparams(("parallel",)),
        name="resnorm",
    )(h, y, g_post.reshape(1, D), *[g.reshape(1, D) for g in g_next])
    return outs[0], list(outs[1:])


def _normmix_kernel(h_ref, halo_ref, g_ref, shift_ref, mix_ref, *out_refs, tt):
    t = pl.program_id(1)
    g = g_ref[...]
    xn = _rms(h_ref[0], g)
    halo = _rms(halo_ref[0], g)
    prev_row = jnp.where(t == 0, shift_ref[0], halo[7:8, :])
    row = lax.broadcasted_iota(jnp.int32, (tt, 1), 0)
    x_prev = jnp.where(row == 0, prev_row, pltpu.roll(xn, 1, axis=0))
    xx = x_prev - xn
    for i in range(6):
        out_refs[i][0] = (xn + xx * mix_ref[i:i + 1, :]).astype(BF16)
    out_refs[6][0] = xn[tt - 1:tt, :]


def _normmix(h, g, shift0, mix):
    B, T, D = h.shape
    tt = _pick(T, (64, 48, 32, 16))
    nb = tt // 8
    tok = pl.BlockSpec((1, tt, D), lambda b, t: (b, t, 0))
    outs = pl.pallas_call(
        functools.partial(_normmix_kernel, tt=tt),
        grid=(B, T // tt),
        in_specs=[tok,
                  pl.BlockSpec((1, 8, D), lambda b, t: (b, jnp.maximum(t * nb - 1, 0), 0)),
                  pl.BlockSpec((1, D), lambda b, t: (0, 0)),
                  pl.BlockSpec((1, 1, D), lambda b, t: (b, 0, 0)),
                  pl.BlockSpec((6, D), lambda b, t: (0, 0))],
        out_specs=[tok] * 6 + [pl.BlockSpec((1, 1, D), lambda b, t: (b, 0, 0))],
        out_shape=[jax.ShapeDtypeStruct((B, T, D), BF16)] * 6 + [jax.ShapeDtypeStruct((B, 1, D), F32)],
        compiler_params=_cparams(("parallel", "arbitrary")),
        name="normmix",
    )(h, h, g.reshape(1, D), shift0.reshape(B, 1, D), mix)
    return list(outs[:6]), outs[6].reshape(B, D)


def _wkv_phase1(c0, n_valid, lanes, refs, prm, bones):
    r_ref, k_ref, v_ref, ld_ref, a_ref = refs
    kkw, kaw, rkw = prm
    G, LW, C = WKV_GROUP, WKV_LANES, WKV_CHUNK
    R = G * C
    sl = pl.ds(c0, C)
    r = r_ref[0, sl, lanes]
    k = k_ref[0, sl, lanes]
    v = v_ref[0, sl, lanes]
    ld = ld_ref[0, sl, lanes]
    a = a_ref[0, sl, lanes]
    if n_valid < C:
        valid = lax.broadcasted_iota(jnp.int32, (C, 1), 0) < n_valid
        r, k, v, ld = (jnp.where(valid, x, 0.0) for x in (r, k, v, ld))

    kk = k * kkw
    km = k * (1.0 + (a - 1.0) * kaw)
    ti = lax.broadcasted_iota(jnp.int32, (C, C), 0)
    si = lax.broadcasted_iota(jnp.int32, (C, C), 1)
    sums = _split_rows_dot([kk * kk, r * km * rkw], bones, 2)
    kk_ss, rk_sum = sums[0], sums[1]
    tri = jnp.where(ti >= si, 1.0, 0.0).astype(BF16)
    cum3 = jnp.dot(tri, jnp.concatenate(_split(ld, 3), axis=1), preferred_element_type=F32)
    cum = cum3[:, :LW] + cum3[:, LW:2 * LW] + cum3[:, 2 * LW:]
    yield
    kk = kk / jnp.maximum(jnp.sqrt(kk_ss), 1e-12)
    bonus = rk_sum * v
    b = kk * a
    cum_last = cum[C - 1:C, :]
    e_neg = jnp.exp(-cum)
    e_rem = jnp.exp(cum_last - cum)
    lane_head = lax.broadcasted_iota(jnp.int32, (1, LW), 1) // HEAD_A

    def stack(x):
        return jnp.concatenate([jnp.where(lane_head == h, x, 0.0) for h in range(G)], axis=0)

    r_s = stack(jnp.exp(cum) * r)
    a_b = stack(jnp.exp(cum - ld) * (-kk)).astype(BF16)
    v_b = stack(v).astype(BF16)
    ar_b = jnp.concatenate([a_b, r_s.astype(BF16)], axis=0)
    bk_b = jnp.concatenate([stack(e_neg * b), stack(e_neg * km)], axis=0).astype(BF16)
    bkh_t = jnp.concatenate([stack(e_rem * b).T, stack(e_rem * km).T], axis=1).astype(BF16)

    big = _bdot_nt(ar_b, bk_b)
    yield
    row = lax.broadcasted_iota(jnp.int32, (R, R), 0)
    col = lax.broadcasted_iota(jnp.int32, (R, R), 1)
    same = (row // C) == (col // C)
    strict = same & (row > col)
    incl = same & (row >= col)
    n_ab = jnp.where(strict, big[:R, :R], 0.0)
    a_ak = jnp.where(strict, big[:R, R:], 0.0)
    a_rb = jnp.where(incl, big[R:, :R], 0.0).astype(BF16)
    a_rk = jnp.where(incl, big[R:, R:], 0.0)
    av = _bdot(jnp.concatenate([a_ak, a_rk], axis=0), v_b)

    tinv = jnp.where(row == col, 1.0, 0.0) + n_ab
    npow = n_ab
    span = 2
    while span < C:
        npow_b = npow.astype(BF16)
        if span > 2:
            both = _bdot(jnp.concatenate([npow_b, tinv.astype(BF16)], axis=0), npow_b)
            npow, tinv = both[:R], tinv + both[R:]
        else:
            npow = _bdot(npow_b, npow_b)
        span *= 2
        yield
    tinv = tinv + _bdot(tinv, npow)
    yield

    wu_b = _bdot(tinv, jnp.concatenate([a_b, av[:R].astype(BF16)], axis=1)).astype(BF16)
    yield
    qo = _bdot(a_rb, wu_b)
    low = jnp.concatenate([jnp.zeros((R, LW), BF16), v_b], axis=1)
    m = _bdot(bkh_t, jnp.concatenate([wu_b, low], axis=0))
    yield
    q_b = (r_s + qo[:, :LW]).astype(BF16)
    o0 = qo[:, LW:] + av[R:]
    li = lax.broadcasted_iota(jnp.int32, (LW, LW), 0)
    lj = lax.broadcasted_iota(jnp.int32, (LW, LW), 1)
    m1_b = (jnp.where(li == lj, jnp.exp(cum_last), 0.0) + m[:, :LW]).astype(BF16)
    return jnp.concatenate([q_b, m1_b], axis=0), o0, m[:, LW:], bonus


def _wkv_phase2(state, ph, c0, n_valid, lanes, g_ref, y_ref, lnw, lnb, bones):
    qm_b, o0, m2, bonus = ph
    C = WKV_CHUNK
    R = WKV_GROUP * C
    qs = _bdot(qm_b, state)
    yield
    o_st = qs[:R] + o0
    new_state = qs[R:] + m2
    o = o_st[0:C, :]
    for h in range(1, WKV_GROUP):
        o = o + o_st[h * C:(h + 1) * C, :]
    mu = _split_rows_dot([o], bones, 2)[0] * (1.0 / HEAD_A)
    yield
    d = o - mu
    var = _split_rows_dot([d * d], bones, 2)[0] * (1.0 / HEAD_A)
    yield
    o = d * lax.rsqrt(var + GN_EPS) * lnw + lnb + bonus
    sl = pl.ds(c0, n_valid)
    y_ref[0, sl, lanes] = (o[:n_valid] * g_ref[0, sl, lanes]).astype(y_ref.dtype)
    return new_state


def _wkv_kernel(r_ref, k_ref, v_ref, ld_ref, a_ref, g_ref, kkw_ref, kaw_ref, rkw_ref, lnw_ref, lnb_ref,
                s0_ref, y_ref, sout_ref, *, T, n_tiles, n_par):
    G, LW, N = WKV_GROUP, WKV_LANES, HEAD_A
    li = lax.broadcasted_iota(jnp.int32, (LW, LW), 0)
    lj = lax.broadcasted_iota(jnp.int32, (LW, LW), 1)
    bones = jnp.where((li // N) == (lj // N), 1.0, 0.0).astype(BF16)
    refs = (r_ref, k_ref, v_ref, ld_ref, a_ref)
    lane_sl = [slice(p * LW, (p + 1) * LW) for p in range(n_tiles)]
    prm = [(kkw_ref[:, s], kaw_ref[:, s], rkw_ref[:, s]) for s in lane_sl]
    post = [(lnw_ref[:, s], lnb_ref[:, s]) for s in lane_sl]

    def steps(c_base, n_valid, n_chunks, states):
        C = WKV_CHUNK
        ph = _interleave([_wkv_phase1(c_base + u * C, n_valid, lane_sl[p], refs, prm[p], bones)
                          for u in range(n_chunks) for p in range(n_tiles)])
        states = list(states)
        for u in range(n_chunks):
            states = _interleave([
                _wkv_phase2(states[p].astype(BF16), ph[u * n_tiles + p], c_base + u * C, n_valid, lane_sl[p],
                            g_ref, y_ref, post[p][0], post[p][1], bones) for p in range(n_tiles)])
        return tuple(states)

    zero = jnp.zeros((N, N), F32)
    states = []
    for p in range(n_tiles):
        blocks = [jnp.concatenate([s0_ref[0, p * G + h].T if j == h else zero for j in range(G)], axis=1)
                  for h in range(G)]
        states.append(jnp.concatenate(blocks, axis=0))
    states = tuple(states)

    head = T % WKV_CHUNK
    if head:
        states = steps(0, head, 1, states)
    n_main = T // WKV_CHUNK
    if n_main:
        def body(i, st):
            return steps(pl.multiple_of(head + i * (n_par * WKV_CHUNK), 16), WKV_CHUNK, n_par, st)

        states = lax.fori_loop(0, n_main // n_par, body, states)

    for p in range(n_tiles):
        for h in range(G):
            sout_ref[0, p * G + h] = states[p][h * N:(h + 1) * N, h * N:(h + 1) * N].T


def _wkv(r, k, v, ld, a, g, kkw, kaw, rkw, lnw, lnb, s0):
    B, T, D = r.shape
    LW, G = WKV_LANES, WKV_GROUP
    n_main = T // WKV_CHUNK
    assert T >= WKV_CHUNK
    n_par = _pick(n_main, (4, 2, 1))
    n_tiles = _pick(D // LW, (2,)) if n_main > 1 else _pick(D // LW, (8, 4, 2))
    bw = n_tiles * LW
    tok = pl.BlockSpec((1, T, bw), lambda b, j: (b, 0, j))
    vec = pl.BlockSpec((1, bw), lambda b, j: (0, j))
    st = pl.BlockSpec((1, n_tiles * G, HEAD_A, HEAD_A), lambda b, j: (b, j, 0, 0))
    y, s_out = pl.pallas_call(
        functools.partial(_wkv_kernel, T=T, n_tiles=n_tiles, n_par=n_par),
        grid=(B, D // bw),
        in_specs=[tok] * 6 + [vec] * 5 + [st],
        out_specs=[tok, st],
        out_shape=[jax.ShapeDtypeStruct((B, T, D), BF16), jax.ShapeDtypeStruct(s0.shape, F32)],
        compiler_params=_cparams(("parallel", "parallel")),
        name="wkv7",
    )(r, k, v, ld, a, g, *[p.reshape(1, D) for p in (kkw, kaw, rkw, lnw, lnb)], s0)
    return y, s_out


def _sb_consts(nk):
    ji = lax.broadcasted_iota(jnp.int32, (nk, nk), 0)
    si = lax.broadcasted_iota(jnp.int32, (nk, nk), 1)
    later = jnp.where(ji > si, -1.0, 0.0).astype(BF16)
    return jnp.concatenate([later, later], axis=0)


def _sb_pair(q, kt, vt, carry, acc, later2, read):
    nq, nk = q.shape[0], kt.shape[0]
    z = _bdot_nt(q, kt) * (HEAD_B ** -0.5)
    yield
    sp = jnp.maximum(z, 0.0) + jnp.log(1.0 + jnp.exp2(jnp.abs(z) * (-LOG2E)))
    spm = sp if read is None else jnp.where(read, sp, 0.0)
    hi, lo = _split(spm, 2)
    if nk % V7X_LANES == 0:
        local = jnp.dot(jnp.concatenate([hi, lo], axis=1), later2, preferred_element_type=F32)
    else:
        local = (jnp.dot(hi, later2[:nk], preferred_element_type=F32)
                 + jnp.dot(lo, later2[:nk], preferred_element_type=F32))
    yield
    total = jnp.broadcast_to(local[:, 0:1] - spm[:, 0:1], (nq, V7X_LANES))
    if nk % V7X_LANES == 0:
        after = local + jnp.concatenate([carry] * (nk // V7X_LANES), axis=1)
    else:
        after = local + carry[:, :nk]
    w = jnp.exp((z - sp) + after)
    if read is not None:
        w = jnp.where(read, w, 0.0)
    return carry + total, acc + _bdot(w, vt)


def _attn_kernel(*refs, T, P, KN, KP):
    if P:
        q_ref, kn_ref, vn_ref, kp_ref, vp_ref, o_ref = refs
    else:
        q_ref, kn_ref, vn_ref, o_ref = refs
    TQ, HB = ATT_TQ, HEAD_B
    n_heads = q_ref.shape[2] // HB
    lanes = [slice(h * HB, (h + 1) * HB) for h in range(n_heads)]
    n_full, rem = T // TQ, T % TQ
    c_new = _sb_consts(KN)
    c_past = _sb_consts(KP) if P else None

    def sweep(qs, ca, n_new):
        def step(k_ref, v_ref, width, consts, jj, ca):
            rows = pl.ds(pl.multiple_of(jj * width, width), width)
            return tuple(_interleave([_sb_pair(qs[h], k_ref[0, rows, lanes[h]].astype(BF16),
                                               v_ref[0, rows, lanes[h]].astype(BF16),
                                               ca[h][0], ca[h][1], consts, None) for h in range(n_heads)]))

        ca = lax.fori_loop(0, n_new, lambda j, c: step(kn_ref, vn_ref, KN, c_new, n_new - 1 - j, c), ca)
        if P:
            n_past = P // KP
            ca = lax.fori_loop(0, n_past, lambda j, c: step(kp_ref, vp_ref, KP, c_past, n_past - 1 - j, c), ca)
        return ca

    def query_tile(q0, nq, k0, nk, off, consts):
        qrows = pl.ds(q0, nq)
        krows = pl.ds(k0, nk)
        qs = [q_ref[0, qrows, lanes[h]] for h in range(n_heads)]
        cmr = lax.broadcasted_iota(jnp.int32, (nq, nk), 1) - lax.broadcasted_iota(jnp.int32, (nq, nk), 0)
        read = cmr < off
        zc = jnp.zeros((nq, V7X_LANES), F32)
        za = jnp.zeros((nq, HB), F32)
        ca = tuple(_interleave([_sb_pair(qs[h], kn_ref[0, krows, lanes[h]], vn_ref[0, krows, lanes[h]],
                                         zc, za, consts, read) for h in range(n_heads)]))
        ca = sweep(qs, ca, k0 // KN)
        for h in range(n_heads):
            o_ref[0, qrows, lanes[h]] = ca[h][1].astype(o_ref.dtype)

    if n_full:
        def full_tile(i, c):
            q0 = pl.multiple_of(i * TQ, TQ)
            k0 = pl.multiple_of((q0 // KN) * KN, KN)
            query_tile(q0, TQ, k0, KN, q0 - k0, c_new)
            return c

        lax.fori_loop(0, n_full, full_tile, 0)
    if rem:
        q0 = n_full * TQ
        query_tile(q0, rem, q0, rem, 0, _sb_consts(rem))


def _attention(q, k_new, v_new, k_past, v_past):
    B, T, D = q.shape
    P = 0 if k_past is None else k_past.shape[1]
    full = T - T % ATT_TQ
    KN = V7X_MXU_DIM if full % V7X_MXU_DIM == 0 and full else ATT_TQ
    KP = V7X_MXU_DIM if P % V7X_MXU_DIM == 0 else ATT_TQ
    assert P % KP == 0 and full % KN == 0
    bw = HEAD_B * _pick(D // HEAD_B, (ATT_HEADS, 2, 1))
    new = pl.BlockSpec((1, T, bw), lambda b, h: (b, 0, h))
    in_specs = [new, new, new]
    args = [q, k_new, v_new]
    if P:
        past = pl.BlockSpec((1, P, bw), lambda b, h: (b, 0, h))
        in_specs += [past, past]
        args += [k_past, v_past]
    return pl.pallas_call(
        functools.partial(_attn_kernel, T=T, P=P, KN=KN, KP=KP),
        grid=(B, D // bw),
        in_specs=in_specs,
        out_specs=new,
        out_shape=jax.ShapeDtypeStruct((B, T, D), BF16),
        compiler_params=_cparams(("parallel", "parallel")),
        name="sb_attention",
    )(*args)


def _ffn_up_kernel(x_ref, wg_ref, wu_ref, st_ref, cw_ref, cb_ref, act_ref, nst_ref, *, T, spt, subs):
    tm, tn = act_ref.shape
    wg = _w_chunks(wg_ref)
    wu = _w_chunks(wu_ref)
    cw = cw_ref[...]
    cb = cb_ref[...]
    st = st_ref[...]
    if spt == 1:
        s0, s1 = st[0, 0:1, :], st[0, 1:2, :]
    else:
        s0 = jnp.broadcast_to(st[:, 0:1, :], (spt, T, tn)).reshape(tm, tn)
        s1 = jnp.broadcast_to(st[:, 1:2, :], (spt, T, tn)).reshape(tm, tn)
    prev = None
    r0 = 0
    for rs in subs:
        rows = slice(r0, r0 + rs)
        lrow = lax.broadcasted_iota(jnp.int32, (rs, 1), 0)
        gate = _dot_chunks(x_ref, rows, wg)
        up = _dot_chunks(x_ref, rows, wu)
        g1 = pltpu.roll(gate, 1, axis=0)
        g2 = pltpu.roll(gate, 2, axis=0)
        if prev is not None:
            g1 = jnp.where(lrow == 0, prev[1:2, :], g1)
            g2 = jnp.where(lrow == 0, prev[0:1, :], jnp.where(lrow == 1, prev[1:2, :], g2))
        pos = (lrow + r0) % T
        g1 = jnp.where(pos == 0, s1, g1)
        g2 = jnp.where(pos == 0, s0, jnp.where(pos == 1, s1, g2))
        conv = g2 * cw[0:1, :] + g1 * cw[1:2, :] + gate * cw[2:3, :] + cb
        act_ref[rows, :] = (jax.nn.silu(conv) * up).astype(act_ref.dtype)
        prev = gate[rs - 2:rs, :]
        r0 += rs
        if spt > 1:
            nst_ref[...] = gate.reshape(spt, T, tn)[:, T - 2:T, :]
    if spt == 1:
        nst_ref[0] = prev


def _ffn_up(x, w_up, layer, conv_state, conv_w, conv_b, T):
    M, D = x.shape
    F = w_up.shape[-1] // 2
    n_seq = M // T
    spt = _pick(n_seq, (8, 4, 2, 1)) if T * 8 <= 512 else 1
    tm = spt * T
    subs = [tm]
    if spt == 1 and tm >= 512 and tm % 16 == 0:
        subs, rem = [], tm
        while rem > 640:
            subs.append(512)
            rem -= 512
        subs += [rem - 128, 128] if rem > 256 else [rem]
    tn = _pick(F, (256, 128))
    nf = F // tn
    x_mode = pl.Buffered(1) if tm * D * 2 > 8 * 1024 * 1024 else None
    st_spec = pl.BlockSpec((spt, 2, tn), lambda i, j: (i, 0, j))
    return pl.pallas_call(
        functools.partial(_ffn_up_kernel, T=T, spt=spt, subs=tuple(subs)),
        grid=(M // tm, nf),
        in_specs=[pl.BlockSpec((tm, D), lambda i, j: (i, 0), pipeline_mode=x_mode),
                  pl.BlockSpec((None, D, tn), lambda i, j: (layer, 0, j)),
                  pl.BlockSpec((None, D, tn), lambda i, j: (layer, 0, j + nf)),
                  st_spec,
                  pl.BlockSpec((CONV_W, tn), lambda i, j: (0, j)),
                  pl.BlockSpec((1, tn), lambda i, j: (0, j))],
        out_specs=[pl.BlockSpec((tm, tn), lambda i, j: (i, j)), st_spec],
        out_shape=[jax.ShapeDtypeStruct((M, F), BF16), jax.ShapeDtypeStruct((n_seq, 2, F), F32)],
        compiler_params=_cparams(("parallel", "arbitrary")),
        name="ffn_up_conv",
    )(x, w_up, w_up, conv_state, conv_w, conv_b.reshape(1, F))


def _run_group(h, shift0, wkv0, conv0, past_k, past_v, p, depth, n_a):
    B, T, D = h.shape
    M = B * T
    flat = lambda t: t.reshape(M, D)
    seq = lambda t: t.reshape(B, T, D)
    gains = p['norm_gains']
    new_wkv, new_shift, new_conv = [], [], []
    v_first = None
    k_new = v_new = k_bf = v_bf = None
    h = flat(h)
    xin = None
    xkv = None
    for layer in range(depth):
        g = gains[layer]
        if layer < n_a:
            i = layer
            mixed_in, last_row = _normmix(seq(h), g[0], shift0[i], p['a_mix'][i])
            xr, xw, xk, xv, xa, xg = (flat(t) for t in mixed_in)
            new_shift.append(last_row)
            r = _matmul(xr, p['a_w_rkv'], widx=(i, 0), out_dtype=F32, name="rwkv_r")
            k = _matmul(xk, p['a_w_rkv'], widx=(i, 1), out_dtype=F32, name="rwkv_k")
            v = _matmul(xv, p['a_w_rkv'], widx=(i, 2), out_dtype=F32, name="rwkv_v")
            ld = _matmul(_matmul(xw, p['a_w1'], widx=(i,), out_dtype=BF16, epilogue=_epi_tanh, name="lora_w1"),
                         p['a_w2'], widx=(i,), out_dtype=F32, epilogue=_epi_log_decay,
                         rows=(p['a_w0'][i].reshape(1, D),), name="lora_w2")
            a = _matmul(_matmul(xa, p['a_a1'], widx=(i,), out_dtype=BF16, name="lora_a1"),
                        p['a_a2'], widx=(i,), out_dtype=F32, epilogue=_epi_bias_sigmoid,
                        rows=(p['a_a0'][i].reshape(1, D),), name="lora_a2")
            gate = _matmul(_matmul(xg, p['a_g1'], widx=(i,), out_dtype=BF16, epilogue=_epi_sigmoid,
                                   name="lora_g1"),
                           p['a_g2'], widx=(i,), out_dtype=F32, name="lora_g2")
            if i == 0:
                v_first = v
            else:
                v = _matmul(_matmul(xv, p['a_v1'], widx=(i - 1,), out_dtype=BF16, name="lora_v1"),
                            p['a_v2'], widx=(i - 1,), out_dtype=F32, epilogue=_epi_value_mix,
                            rows=(p['a_v0'][i - 1].reshape(1, D),), tiles=(v, v_first), name="lora_v2")
            y, s_out = _wkv(seq(r), seq(k), seq(v), seq(ld), seq(a), seq(gate),
                            p['a_k_k'][i], p['a_k_a'][i], p['a_r_k'][i].reshape(D), p['a_ln_w'][i],
                            p['a_ln_b'][i], wkv0[i])
            new_wkv.append(s_out)
            mixed = _matmul(flat(y), p['a_w_out'], widx=(i,), out_dtype=F32, name="rwkv_out")
        else:
            j = layer - n_a
            if layer == n_a:
                k_new, k_bf = _matmul(xkv, p['w_kv'], out_dtype=(F32, BF16), n=D, name="kv_proj_k")
                v_new, v_bf = _matmul(xkv, p['w_kv'], out_dtype=(F32, BF16), n=D, col_off=D, name="kv_proj_v")
            q = _matmul(xin, p['b_w_q'], widx=(j,), out_dtype=BF16, name="attn_q")
            attn = _attention(seq(q), seq(k_bf), seq(v_bf), past_k, past_v)
            mixed = _matmul(flat(attn), p['b_w_out'], widx=(j,), out_dtype=F32, name="attn_out")
        h, (xf,) = _resnorm(h, mixed, g[1], [g[2]])
        act, conv_new = _ffn_up(xf, p['ffn_w_up'], layer, conv0[layer], p['ffn_conv_w'][layer],
                                p['ffn_conv_b'][layer], T)
        new_conv.append(conv_new)
        f = _matmul(act, p['ffn_w_down'], widx=(layer,), out_dtype=F32, name="ffn_down")
        nxt = []
        if layer + 1 < depth and layer + 1 >= n_a:
            nxt.append(gains[layer + 1][0])
            if layer + 1 == n_a:
                nxt.append(p['kv_norm'])
        h, normed = _resnorm(h, f, g[3], nxt)
        if normed:
            xin = normed[0]
            if len(normed) > 1:
                xkv = normed[1]
    return (seq(h), seq(k_new), seq(v_new), jnp.stack(new_wkv), jnp.stack(new_shift),
            jnp.stack(new_conv))


def kernel(x_prompt, x_sample, cache_k, cache_v, state_wkv, state_shift, state_ffn_conv, meta_tokens, norm_gains, a_mix, a_w_rkv, a_w_out, a_w0, a_w1, a_w2, a_a0, a_a1, a_a2, a_v0, a_v1, a_v2, a_g1, a_g2, a_k_k, a_k_a, a_r_k, a_ln_w, a_ln_b, kv_norm, w_kv, b_w_q, b_w_out, ffn_w_up, ffn_conv_w, ffn_conv_b, ffn_w_down):
    B, S, D = x_prompt.shape
    Bs, Ts, _ = x_sample.shape
    depth = norm_gains.shape[0]
    n_a = a_mix.shape[0]
    F = ffn_w_down.shape[1]
    H_a, H_b = D // HEAD_A, D // HEAD_B
    assert n_a >= 1 and depth > n_a
    bf = lambda w: w.astype(BF16)
    p = {
        'norm_gains': norm_gains, 'a_mix': a_mix, 'a_w_rkv': a_w_rkv, 'a_w_out': a_w_out,
        'a_w0': a_w0, 'a_w1': a_w1, 'a_w2': a_w2, 'a_a0': a_a0, 'a_a1': a_a1,
        'a_a2': a_a2, 'a_v0': a_v0, 'a_v1': a_v1, 'a_v2': a_v2, 'a_g1': a_g1,
        'a_g2': a_g2, 'a_k_k': a_k_k, 'a_k_a': a_k_a, 'a_r_k': a_r_k, 'a_ln_w': a_ln_w,
        'a_ln_b': a_ln_b, 'kv_norm': kv_norm, 'w_kv': w_kv, 'b_w_q': b_w_q,
        'b_w_out': b_w_out, 'ffn_w_up': ffn_w_up, 'ffn_conv_w': ffn_conv_w,
        'ffn_conv_b': ffn_conv_b, 'ffn_w_down': bf(ffn_w_down),
    }
    dt = x_prompt.dtype
    T = N_META + S
    meta = jnp.broadcast_to(meta_tokens[None].astype(dt), (B, N_META, D))
    h0 = jnp.concatenate([meta, x_prompt], axis=1)
    h_p, k_p, v_p, wkv_p, shift_p, conv_p = _run_group(
        h0, jnp.zeros((n_a, B, D), dt), jnp.zeros((n_a, B, H_a, HEAD_A, HEAD_A), dt),
        jnp.zeros((depth, B, CONV_W - 1, F), dt), None, None, p, depth, n_a)
    P = cache_k.shape[1]
    h_s, k_s, v_s, wkv_s, shift_s, conv_s = _run_group(
        x_sample, state_shift, state_wkv, state_ffn_conv, cache_k.reshape(Bs, P, D),
        cache_v.reshape(Bs, P, D), p, depth, n_a)
    heads = lambda t: t.reshape(t.shape[0], t.shape[1], H_b, HEAD_B)
    return (h_p[:, N_META:], h_s, heads(k_p), heads(v_p), heads(k_s), heads(v_s),
            wkv_p, wkv_s, shift_p, shift_s, conv_p, conv_s)
```

```python
import functools

import jax
import jax.numpy as jnp
from jax import lax
from jax.experimental import pallas as pl
from jax.experimental.pallas import tpu as pltpu

F32 = jnp.float32
BF16 = jnp.bfloat16

N_META = 16
HEAD_A = 64
HEAD_B = 128
GN_EPS = 64e-5
RMS_EPS = 1e-6
CONV_W = 3

V7X_LANES = 128
V7X_MXU_DIM = 256
V7X_VMEM_BYTES = 64 * 1024 * 1024
VMEM_LIMIT = V7X_VMEM_BYTES - 8 * 1024 * 1024

W_CAST_CHUNK = 1024
KGRID_W_BYTES = 8 * 1024 * 1024
LOG2E = 1.4426950408889634
EXP_NEG_HALF = -0.6065306597126334

WKV_GROUP = 2
WKV_LANES = WKV_GROUP * HEAD_A
WKV_CHUNK = 64
ATT_TQ = 128
ATT_HEADS = 8


def _pick(n, prefs):
    for p in prefs:
        if p <= n and n % p == 0:
            return p
    return n


def _cparams(sem):
    return pltpu.CompilerParams(dimension_semantics=sem, vmem_limit_bytes=VMEM_LIMIT)


def _bdot(a, b):
    return jnp.dot(a.astype(BF16), b.astype(BF16), preferred_element_type=F32)


def _bdot_nt(a, b):
    return lax.dot_general(a.astype(BF16), b.astype(BF16), (((1,), (1,)), ((), ())),
                           preferred_element_type=F32)


def _bdot_tn(a, b):
    return lax.dot_general(a.astype(BF16), b.astype(BF16), (((0,), (0,)), ((), ())),
                           preferred_element_type=F32)


def _split(x, parts):
    out = []
    rem = x
    for i in range(parts):
        hi = rem.astype(BF16)
        out.append(hi)
        if i + 1 < parts:
            rem = rem - hi.astype(F32)
    return out


def _split_dot(x, m_bf16, parts):
    acc = None
    for hi in _split(x, parts):
        t = jnp.dot(hi, m_bf16, preferred_element_type=F32)
        acc = t if acc is None else acc + t
    return acc


def _split_rows_dot(xs, m_bf16, parts):
    n = xs[0].shape[0]
    stacked = jnp.concatenate([p for x in xs for p in _split(x, parts)], axis=0)
    res = jnp.dot(stacked, m_bf16, preferred_element_type=F32)
    out = []
    for i in range(len(xs)):
        acc = res[i * parts * n:(i * parts + 1) * n]
        for j in range(1, parts):
            acc = acc + res[(i * parts + j) * n:(i * parts + j + 1) * n]
        out.append(acc)
    return out


def _split_dot_lhs(m_bf16, x, parts):
    acc = None
    for hi in _split(x, parts):
        t = jnp.dot(m_bf16, hi, preferred_element_type=F32)
        acc = t if acc is None else acc + t
    return acc


def _interleave(gens):
    results = [None] * len(gens)
    live = list(range(len(gens)))
    while live:
        still = []
        for i in live:
            try:
                next(gens[i])
                still.append(i)
            except StopIteration as stop:
                results[i] = stop.value
        live = still
    return results


def _rms(x, g):
    return x * lax.rsqrt(jnp.mean(x * x, axis=-1, keepdims=True) + RMS_EPS) * g


def _epi_none(acc):
    return acc


def _epi_tanh(acc):
    return jnp.tanh(acc)


def _epi_sigmoid(acc):
    return jax.nn.sigmoid(acc)


def _epi_bias_sigmoid(acc, bias):
    return jax.nn.sigmoid(bias + acc)


def _epi_log_decay(acc, w0):
    return EXP_NEG_HALF * jax.nn.sigmoid(w0 + acc)


def _epi_value_mix(acc, v0, v, v_first):
    return v + (v_first - v) * jax.nn.sigmoid(v0 + acc)


def _w_chunks(w_ref):
    K = w_ref.shape[0]
    if w_ref.dtype == BF16 or K % W_CAST_CHUNK or K == W_CAST_CHUNK:
        return [(slice(None), w_ref[...].astype(BF16))]
    return [(slice(k0, k0 + W_CAST_CHUNK), w_ref[k0:k0 + W_CAST_CHUNK, :].astype(BF16))
            for k0 in range(0, K, W_CAST_CHUNK)]


def _dot_chunks(x_ref, rows, chunks):
    acc = None
    for ks, w in chunks:
        t = jnp.dot(x_ref[rows, ks], w, preferred_element_type=F32)
        acc = t if acc is None else acc + t
    return acc


def _dot_cast_w(x_ref, w_ref):
    return _dot_chunks(x_ref, slice(None), _w_chunks(w_ref))


def _mm_kernel(*refs, epilogue, n_row, n_tile):
    x_ref, w_ref = refs[0], refs[1]
    rows = [r[...] for r in refs[2:2 + n_row]]
    tiles = [r[...] for r in refs[2 + n_row:2 + n_row + n_tile]]
    acc = _dot_cast_w(x_ref, w_ref)
    res = epilogue(acc, *rows, *tiles)
    for o_ref in refs[2 + n_row + n_tile:]:
        o_ref[...] = res.astype(o_ref.dtype)


def _matmul_tiles(M, K, N, w_bytes, out_bytes, n_tiles):
    budget = VMEM_LIMIT - 6 * 1024 * 1024
    row_tiles = [t for t in (2064, 1376, 688, 512, 256, 128, 64, 16) if M % t == 0] or [M]
    tn_opts = [t for t in (512, 256, 128) if N % t == 0] or [N]

    def fits(tm, tn, x_bufs):
        need = (x_bufs * tm * K * 2 + 2 * K * tn * w_bytes + (W_CAST_CHUNK * tn * 2 if w_bytes > 2 else 0)
                + 2 * tm * tn * (out_bytes + 4 * n_tiles) + tm * tn * 4)
        return need <= budget

    for tn in tn_opts:
        for tm in row_tiles:
            if tm >= min(1024, row_tiles[0]) and fits(tm, tn, 2):
                return tm, tn, 2
        for tm in row_tiles:
            if fits(tm, tn, 1):
                return tm, tn, 1
    return row_tiles[-1], tn_opts[-1], 1


def _mmk_kernel(*refs, epilogue, n_row, n_tile, n_out):
    x_ref, w_ref = refs[0], refs[1]
    row_refs = refs[2:2 + n_row]
    tile_refs = refs[2 + n_row:2 + n_row + n_tile]
    out_refs = refs[2 + n_row + n_tile:2 + n_row + n_tile + n_out]
    acc_ref = refs[-1]
    k = pl.program_id(0)
    part = jnp.dot(x_ref[...], w_ref[...].astype(BF16), preferred_element_type=F32)

    @pl.when(k == 0)
    def _():
        acc_ref[...] = part

    @pl.when(k > 0)
    def _():
        acc_ref[...] += part

    @pl.when(k == pl.num_programs(0) - 1)
    def _():
        res = epilogue(acc_ref[...], *[r[...] for r in row_refs], *[t[...] for t in tile_refs])
        for o_ref in out_refs:
            o_ref[...] = res.astype(o_ref.dtype)


def _matmul_kgrid(x, w, dtypes, epilogue, rows, tiles, widx, N, col_off, name, as_tuple):
    M, K = x.shape
    tk = _pick(K, tuple(t for t in (2048, 1024, 512, 256, 128) if t * N * w.dtype.itemsize <= KGRID_W_BYTES))
    jblk = col_off // N
    full = pl.BlockSpec((M, N), lambda k: (0, 0))
    in_specs = [pl.BlockSpec((M, tk), lambda k: (0, k)),
                pl.BlockSpec((None,) * len(widx) + (tk, N), lambda k: (*widx, k, jblk))]
    in_specs += [pl.BlockSpec((1, N), lambda k: (0, 0)) for _ in rows] + [full for _ in tiles]
    outs = pl.pallas_call(
        functools.partial(_mmk_kernel, epilogue=epilogue, n_row=len(rows), n_tile=len(tiles), n_out=len(dtypes)),
        grid=(K // tk,),
        in_specs=in_specs,
        out_specs=[full for _ in dtypes],
        out_shape=[jax.ShapeDtypeStruct((M, N), d) for d in dtypes],
        scratch_shapes=[pltpu.VMEM((M, N), F32)],
        compiler_params=_cparams(("arbitrary",)),
        name=name,
    )(x, w, *rows, *tiles)
    return outs if as_tuple else outs[0]


def _matmul(x, w, *, out_dtype, epilogue=_epi_none, rows=(), tiles=(), widx=(), n=None, col_off=0, name):
    M, K = x.shape
    N = w.shape[-1] if n is None else n
    dtypes = out_dtype if isinstance(out_dtype, tuple) else (out_dtype,)
    if M <= 512 and w.dtype == F32 and K * N * 4 >= 2 * KGRID_W_BYTES and col_off % N == 0:
        return _matmul_kgrid(x, w, dtypes, epilogue, rows, tiles, widx, N, col_off, name,
                             isinstance(out_dtype, tuple))
    tm, tn, x_bufs = _matmul_tiles(M, K, N, w.dtype.itemsize, sum(jnp.dtype(d).itemsize for d in dtypes),
                                   len(tiles))
    assert col_off % tn == 0 and w.ndim == 2 + len(widx)
    joff = col_off // tn
    x_mode = pl.Buffered(1) if x_bufs == 1 else None
    in_specs = [pl.BlockSpec((tm, K), lambda i, j: (i, 0), pipeline_mode=x_mode),
                pl.BlockSpec((None,) * len(widx) + (K, tn), lambda i, j: (*widx, 0, j + joff))]
    in_specs += [pl.BlockSpec((1, tn), lambda i, j: (0, j)) for _ in rows]
    in_specs += [pl.BlockSpec((tm, tn), lambda i, j: (i, j)) for _ in tiles]
    outs = pl.pallas_call(
        functools.partial(_mm_kernel, epilogue=epilogue, n_row=len(rows), n_tile=len(tiles)),
        grid=(M // tm, N // tn),
        in_specs=in_specs,
        out_specs=[pl.BlockSpec((tm, tn), lambda i, j: (i, j)) for _ in dtypes],
        out_shape=[jax.ShapeDtypeStruct((M, N), d) for d in dtypes],
        compiler_params=_cparams(("parallel", "arbitrary")),
        name=name,
    )(x, w, *rows, *tiles)
    return outs if isinstance(out_dtype, tuple) else outs[0]


def _resnorm_kernel(*refs, n_next):
    h_ref, y_ref, gp_ref = refs[0], refs[1], refs[2]
    gn_refs = refs[3:3 + n_next]
    hn_ref = refs[3 + n_next]
    xo_refs = refs[4 + n_next:]
    hn = h_ref[...] + _rms(y_ref[...], gp_ref[...])
    hn_ref[...] = hn
    for gn_ref, xo_ref in zip(gn_refs, xo_refs):
        xo_ref[...] = _rms(hn, gn_ref[...]).astype(xo_ref.dtype)


def _resnorm(h, y, g_post, g_next):
    M, D = h.shape
    tr = _pick(M, (192, 128, 64, 16))
    n = len(g_next)
    row = pl.BlockSpec((tr, D), lambda i: (i, 0))
    vec = pl.BlockSpec((1, D), lambda i: (0, 0))
    outs = pl.pallas_call(
        functools.partial(_resnorm_kernel, n_next=n),
        grid=(M // tr,),
        in_specs=[row, row, vec] + [vec] * n,
        out_specs=[row] + [row] * n,
        out_shape=[jax.ShapeDtypeStruct((M, D), F32)] + [jax.ShapeDtypeStruct((M, D), BF16)] * n,
        compiler_params=_cparams(("parallel",)),
        name="resnorm",
    )(h, y, g_post.reshape(1, D), *[g.reshape(1, D) for g in g_next])
    return outs[0], list(outs[1:])


def _normmix_kernel(h_ref, halo_ref, g_ref, shift_ref, mix_ref, *out_refs, tt):
    t = pl.program_id(1)
    g = g_ref[...]
    xn = _rms(h_ref[0], g)
    halo = _rms(halo_ref[0], g)
    prev_row = jnp.where(t == 0, shift_ref[0], halo[7:8, :])
    row = lax.broadcasted_iota(jnp.int32, (tt, 1), 0)
    x_prev = jnp.where(row == 0, prev_row, pltpu.roll(xn, 1, axis=0))
    xx = x_prev - xn
    for i in range(6):
        out_refs[i][0] = (xn + xx * mix_ref[i:i + 1, :]).astype(BF16)
    out_refs[6][0] = xn[tt - 1:tt, :]


def _normmix(h, g, shift0, mix):
    B, T, D = h.shape
    tt = _pick(T, (64, 48, 32, 16))
    nb = tt // 8
    tok = pl.BlockSpec((1, tt, D), lambda b, t: (b, t, 0))
    outs = pl.pallas_call(
        functools.partial(_normmix_kernel, tt=tt),
        grid=(B, T // tt),
        in_specs=[tok,
                  pl.BlockSpec((1, 8, D), lambda b, t: (b, jnp.maximum(t * nb - 1, 0), 0)),
                  pl.BlockSpec((1, D), lambda b, t: (0, 0)),
                  pl.BlockSpec((1, 1, D), lambda b, t: (b, 0, 0)),
                  pl.BlockSpec((6, D), lambda b, t: (0, 0))],
        out_specs=[tok] * 6 + [pl.BlockSpec((1, 1, D), lambda b, t: (b, 0, 0))],
        out_shape=[jax.ShapeDtypeStruct((B, T, D), BF16)] * 6 + [jax.ShapeDtypeStruct((B, 1, D), F32)],
        compiler_params=_cparams(("parallel", "arbitrary")),
        name="normmix",
    )(h, h, g.reshape(1, D), shift0.reshape(B, 1, D), mix)
    return list(outs[:6]), outs[6].reshape(B, D)


def _wkv_phase1(c0, n_valid, lanes, refs, prm, bones):
    r_ref, k_ref, v_ref, ld_ref, a_ref = refs
    kkw, kaw, rkw = prm
    G, LW, C = WKV_GROUP, WKV_LANES, WKV_CHUNK
    R = G * C
    sl = pl.ds(c0, C)
    r = r_ref[0, sl, lanes]
    k = k_ref[0, sl, lanes]
    v = v_ref[0, sl, lanes]
    ld = ld_ref[0, sl, lanes]
    a = a_ref[0, sl, lanes]
    if n_valid < C:
        valid = lax.broadcasted_iota(jnp.int32, (C, 1), 0) < n_valid
        r, k, v, ld = (jnp.where(valid, x, 0.0) for x in (r, k, v, ld))

    kk = k * kkw
    km = k * (1.0 + (a - 1.0) * kaw)
    ti = lax.broadcasted_iota(jnp.int32, (C, C), 0)
    si = lax.broadcasted_iota(jnp.int32, (C, C), 1)
    sums = _split_rows_dot([kk * kk, r * km * rkw], bones, 2)
    kk_ss, rk_sum = sums[0], sums[1]
    tri = jnp.where(ti >= si, 1.0, 0.0).astype(BF16)
    cum3 = jnp.dot(tri, jnp.concatenate(_split(ld, 3), axis=1), preferred_element_type=F32)
    cum = cum3[:, :LW] + cum3[:, LW:2 * LW] + cum3[:, 2 * LW:]
    yield
    kk = kk / jnp.maximum(jnp.sqrt(kk_ss), 1e-12)
    bonus = rk_sum * v
    b = kk * a
    cum_last = cum[C - 1:C, :]
    e_neg = jnp.exp(-cum)
    e_rem = jnp.exp(cum_last - cum)
    lane_head = lax.broadcasted_iota(jnp.int32, (1, LW), 1) // HEAD_A

    def stack(x):
        return jnp.concatenate([jnp.where(lane_head == h, x, 0.0) for h in range(G)], axis=0)

    r_s = stack(jnp.exp(cum) * r)
    a_b = stack(jnp.exp(cum - ld) * (-kk)).astype(BF16)
    v_b = stack(v).astype(BF16)
    ar_b = jnp.concatenate([a_b, r_s.astype(BF16)], axis=0)
    bk_b = jnp.concatenate([stack(e_neg * b), stack(e_neg * km)], axis=0).astype(BF16)
    bkh_t = jnp.concatenate([stack(e_rem * b).T, stack(e_rem * km).T], axis=1).astype(BF16)

    big = _bdot_nt(ar_b, bk_b)
    yield
    row = lax.broadcasted_iota(jnp.int32, (R, R), 0)
    col = lax.broadcasted_iota(jnp.int32, (R, R), 1)
    same = (row // C) == (col // C)
    strict = same & (row > col)
    incl = same & (row >= col)
    n_ab = jnp.where(strict, big[:R, :R], 0.0)
    a_ak = jnp.where(strict, big[:R, R:], 0.0)
    a_rb = jnp.where(incl, big[R:, :R], 0.0).astype(BF16)
    a_rk = jnp.where(incl, big[R:, R:], 0.0)
    av = _bdot(jnp.concatenate([a_ak, a_rk], axis=0), v_b)

    tinv = jnp.where(row == col, 1.0, 0.0) + n_ab
    npow = n_ab
    span = 2
    while span < C:
        npow_b = npow.astype(BF16)
        if span > 2:
            both = _bdot(jnp.concatenate([npow_b, tinv.astype(BF16)], axis=0), npow_b)
            npow, tinv = both[:R], tinv + both[R:]
        else:
            npow = _bdot(npow_b, npow_b)
        span *= 2
        yield
    tinv = tinv + _bdot(tinv, npow)
    yield

    wu_b = _bdot(tinv, jnp.concatenate([a_b, av[:R].astype(BF16)], axis=1)).astype(BF16)
    yield
    qo = _bdot(a_rb, wu_b)
    low = jnp.concatenate([jnp.zeros((R, LW), BF16), v_b], axis=1)
    m = _bdot(bkh_t, jnp.concatenate([wu_b, low], axis=0))
    yield
    q_b = (r_s + qo[:, :LW]).astype(BF16)
    o0 = qo[:, LW:] + av[R:]
    li = lax.broadcasted_iota(jnp.int32, (LW, LW), 0)
    lj = lax.broadcasted_iota(jnp.int32, (LW, LW), 1)
    m1_b = (jnp.where(li == lj, jnp.exp(cum_last), 0.0) + m[:, :LW]).astype(BF16)
    return jnp.concatenate([q_b, m1_b], axis=0), o0, m[:, LW:], bonus


def _wkv_phase2(state, ph, c0, n_valid, lanes, g_ref, y_ref, lnw, lnb, bones):
    qm_b, o0, m2, bonus = ph
    C = WKV_CHUNK
    R = WKV_GROUP * C
    qs = _bdot(qm_b, state)
    yield
    o_st = qs[:R] + o0
    new_state = qs[R:] + m2
    o = o_st[0:C, :]
    for h in range(1, WKV_GROUP):
        o = o + o_st[h * C:(h + 1) * C, :]
    mu = _split_rows_dot([o], bones, 2)[0] * (1.0 / HEAD_A)
    yield
    d = o - mu
    var = _split_rows_dot([d * d], bones, 2)[0] * (1.0 / HEAD_A)
    yield
    o = d * lax.rsqrt(var + GN_EPS) * lnw + lnb + bonus
    sl = pl.ds(c0, n_valid)
    y_ref[0, sl, lanes] = (o[:n_valid] * g_ref[0, sl, lanes]).astype(y_ref.dtype)
    return new_state


def _wkv_kernel(r_ref, k_ref, v_ref, ld_ref, a_ref, g_ref, kkw_ref, kaw_ref, rkw_ref, lnw_ref, lnb_ref,
                s0_ref, y_ref, sout_ref, *, T, n_tiles, n_par):
    G, LW, N = WKV_GROUP, WKV_LANES, HEAD_A
    li = lax.broadcasted_iota(jnp.int32, (LW, LW), 0)
    lj = lax.broadcasted_iota(jnp.int32, (LW, LW), 1)
    bones = jnp.where((li // N) == (lj // N), 1.0, 0.0).astype(BF16)
    refs = (r_ref, k_ref, v_ref, ld_ref, a_ref)
    lane_sl = [slice(p * LW, (p + 1) * LW) for p in range(n_tiles)]
    prm = [(kkw_ref[:, s], kaw_ref[:, s], rkw_ref[:, s]) for s in lane_sl]
    post = [(lnw_ref[:, s], lnb_ref[:, s]) for s in lane_sl]

    def steps(c_base, n_valid, n_chunks, states):
        C = WKV_CHUNK
        ph = _interleave([_wkv_phase1(c_base + u * C, n_valid, lane_sl[p], refs, prm[p], bones)
                          for u in range(n_chunks) for p in range(n_tiles)])
        states = list(states)
        for u in range(n_chunks):
            states = _interleave([
                _wkv_phase2(states[p].astype(BF16), ph[u * n_tiles + p], c_base + u * C, n_valid, lane_sl[p],
                            g_ref, y_ref, post[p][0], post[p][1], bones) for p in range(n_tiles)])
        return tuple(states)

    zero = jnp.zeros((N, N), F32)
    states = []
    for p in range(n_tiles):
        blocks = [jnp.concatenate([s0_ref[0, p * G + h].T if j == h else zero for j in range(G)], axis=1)
                  for h in range(G)]
        states.append(jnp.concatenate(blocks, axis=0))
    states = tuple(states)

    head = T % WKV_CHUNK
    if head:
        states = steps(0, head, 1, states)
    n_main = T // WKV_CHUNK
    if n_main:
        def body(i, st):
            return steps(pl.multiple_of(head + i * (n_par * WKV_CHUNK), 16), WKV_CHUNK, n_par, st)

        states = lax.fori_loop(0, n_main // n_par, body, states)

    for p in range(n_tiles):
        for h in range(G):
            sout_ref[0, p * G + h] = states[p][h * N:(h + 1) * N, h * N:(h + 1) * N].T


def _wkv(r, k, v, ld, a, g, kkw, kaw, rkw, lnw, lnb, s0):
    B, T, D = r.shape
    LW, G = WKV_LANES, WKV_GROUP
    n_main = T // WKV_CHUNK
    assert T >= WKV_CHUNK
    n_par = _pick(n_main, (4, 2, 1))
    n_tiles = _pick(D // LW, (2,)) if n_main > 1 else _pick(D // LW, (8, 4, 2))
    bw = n_tiles * LW
    tok = pl.BlockSpec((1, T, bw), lambda b, j: (b, 0, j))
    vec = pl.BlockSpec((1, bw), lambda b, j: (0, j))
    st = pl.BlockSpec((1, n_tiles * G, HEAD_A, HEAD_A), lambda b, j: (b, j, 0, 0))
    y, s_out = pl.pallas_call(
        functools.partial(_wkv_kernel, T=T, n_tiles=n_tiles, n_par=n_par),
        grid=(B, D // bw),
        in_specs=[tok] * 6 + [vec] * 5 + [st],
        out_specs=[tok, st],
        out_shape=[jax.ShapeDtypeStruct((B, T, D), BF16), jax.ShapeDtypeStruct(s0.shape, F32)],
        compiler_params=_cparams(("parallel", "parallel")),
        name="wkv7",
    )(r, k, v, ld, a, g, *[p.reshape(1, D) for p in (kkw, kaw, rkw, lnw, lnb)], s0)
    return y, s_out


def _sb_consts(nk):
    ji = lax.broadcasted_iota(jnp.int32, (nk, nk), 0)
    si = lax.broadcasted_iota(jnp.int32, (nk, nk), 1)
    later = jnp.where(ji > si, -1.0, 0.0).astype(BF16)
    return jnp.concatenate([later, later], axis=0)


def _sb_pair(q, kt, vt, carry, acc, later2, read):
    nq, nk = q.shape[0], kt.shape[0]
    z = _bdot_nt(q, kt) * (HEAD_B ** -0.5)
    yield
    sp = jnp.maximum(z, 0.0) + jnp.log(1.0 + jnp.exp2(jnp.abs(z) * (-LOG2E)))
    spm = sp if read is None else jnp.where(read, sp, 0.0)
    hi, lo = _split(spm, 2)
    if nk % V7X_LANES == 0:
        local = jnp.dot(jnp.concatenate([hi, lo], axis=1), later2, preferred_element_type=F32)
    else:
        local = (jnp.dot(hi, later2[:nk], preferred_element_type=F32)
                 + jnp.dot(lo, later2[:nk], preferred_element_type=F32))
    yield
    total = jnp.broadcast_to(local[:, 0:1] - spm[:, 0:1], (nq, V7X_LANES))
    if nk % V7X_LANES == 0:
        after = local + jnp.concatenate([carry] * (nk // V7X_LANES), axis=1)
    else:
        after = local + carry[:, :nk]
    w = jnp.exp((z - sp) + after)
    if read is not None:
        w = jnp.where(read, w, 0.0)
    return carry + total, acc + _bdot(w, vt)


def _attn_kernel(*refs, T, P, KN, KP):
    if P:
        q_ref, kn_ref, vn_ref, kp_ref, vp_ref, o_ref = refs
    else:
        q_ref, kn_ref, vn_ref, o_ref = refs
    TQ, HB = ATT_TQ, HEAD_B
    n_heads = q_ref.shape[2] // HB
    lanes = [slice(h * HB, (h + 1) * HB) for h in range(n_heads)]
    n_full, rem = T // TQ, T % TQ
    c_new = _sb_consts(KN)
    c_past = _sb_consts(KP) if P else None

    def sweep(qs, ca, n_new):
        def step(k_ref, v_ref, width, consts, jj, ca):
            rows = pl.ds(pl.multiple_of(jj * width, width), width)
            return tuple(_interleave([_sb_pair(qs[h], k_ref[0, rows, lanes[h]].astype(BF16),
                                               v_ref[0, rows, lanes[h]].astype(BF16),
                                               ca[h][0], ca[h][1], consts, None) for h in range(n_heads)]))

        ca = lax.fori_loop(0, n_new, lambda j, c: step(kn_ref, vn_ref, KN, c_new, n_new - 1 - j, c), ca)
        if P:
            n_past = P // KP
            ca = lax.fori_loop(0, n_past, lambda j, c: step(kp_ref, vp_ref, KP, c_past, n_past - 1 - j, c), ca)
        return ca

    def query_tile(q0, nq, k0, nk, off, consts):
        qrows = pl.ds(q0, nq)
        krows = pl.ds(k0, nk)
        qs = [q_ref[0, qrows, lanes[h]] for h in range(n_heads)]
        cmr = lax.broadcasted_iota(jnp.int32, (nq, nk), 1) - lax.broadcasted_iota(jnp.int32, (nq, nk), 0)
        read = cmr < off
        zc = jnp.zeros((nq, V7X_LANES), F32)
        za = jnp.zeros((nq, HB), F32)
        ca = tuple(_interleave([_sb_pair(qs[h], kn_ref[0, krows, lanes[h]], vn_ref[0, krows, lanes[h]],
                                         zc, za, consts, read) for h in range(n_heads)]))
        ca = sweep(qs, ca, k0 // KN)
        for h in range(n_heads):
            o_ref[0, qrows, lanes[h]] = ca[h][1].astype(o_ref.dtype)

    if n_full:
        def full_tile(i, c):
            q0 = pl.multiple_of(i * TQ, TQ)
            k0 = pl.multiple_of((q0 // KN) * KN, KN)
            query_tile(q0, TQ, k0, KN, q0 - k0, c_new)
            return c

        lax.fori_loop(0, n_full, full_tile, 0)
    if rem:
        q0 = n_full * TQ
        query_tile(q0, rem, q0, rem, 0, _sb_consts(rem))


def _attention(q, k_new, v_new, k_past, v_past):
    B, T, D = q.shape
    P = 0 if k_past is None else k_past.shape[1]
    full = T - T % ATT_TQ
    KN = V7X_MXU_DIM if full % V7X_MXU_DIM == 0 and full else ATT_TQ
    KP = V7X_MXU_DIM if P % V7X_MXU_DIM == 0 else ATT_TQ
    assert P % KP == 0 and full % KN == 0
    bw = HEAD_B * _pick(D // HEAD_B, (ATT_HEADS, 2, 1))
    new = pl.BlockSpec((1, T, bw), lambda b, h: (b, 0, h))
    in_specs = [new, new, new]
    args = [q, k_new, v_new]
    if P:
        past = pl.BlockSpec((1, P, bw), lambda b, h: (b, 0, h))
        in_specs += [past, past]
        args += [k_past, v_past]
    return pl.pallas_call(
        functools.partial(_attn_kernel, T=T, P=P, KN=KN, KP=KP),
        grid=(B, D // bw),
        in_specs=in_specs,
        out_specs=new,
        out_shape=jax.ShapeDtypeStruct((B, T, D), BF16),
        compiler_params=_cparams(("parallel", "parallel")),
        name="sb_attention",
    )(*args)


def _ffn_up_kernel(x_ref, wg_ref, wu_ref, st_ref, cw_ref, cb_ref, act_ref, nst_ref, *, T, spt, subs):
    tm, tn = act_ref.shape
    wg = _w_chunks(wg_ref)
    wu = _w_chunks(wu_ref)
    cw = cw_ref[...]
    cb = cb_ref[...]
    st = st_ref[...]
    if spt == 1:
        s0, s1 = st[0, 0:1, :], st[0, 1:2, :]
    else:
        s0 = jnp.broadcast_to(st[:, 0:1, :], (spt, T, tn)).reshape(tm, tn)
        s1 = jnp.broadcast_to(st[:, 1:2, :], (spt, T, tn)).reshape(tm, tn)
    prev = None
    r0 = 0
    for rs in subs:
        rows = slice(r0, r0 + rs)
        lrow = lax.broadcasted_iota(jnp.int32, (rs, 1), 0)
        gate = _dot_chunks(x_ref, rows, wg)
        up = _dot_chunks(x_ref, rows, wu)
        g1 = pltpu.roll(gate, 1, axis=0)
        g2 = pltpu.roll(gate, 2, axis=0)
        if prev is not None:
            g1 = jnp.where(lrow == 0, prev[1:2, :], g1)
            g2 = jnp.where(lrow == 0, prev[0:1, :], jnp.where(lrow == 1, prev[1:2, :], g2))
        pos = (lrow + r0) % T
        g1 = jnp.where(pos == 0, s1, g1)
        g2 = jnp.where(pos == 0, s0, jnp.where(pos == 1, s1, g2))
        conv = g2 * cw[0:1, :] + g1 * cw[1:2, :] + gate * cw[2:3, :] + cb
        act_ref[rows, :] = (jax.nn.silu(conv) * up).astype(act_ref.dtype)
        prev = gate[rs - 2:rs, :]
        r0 += rs
        if spt > 1:
            nst_ref[...] = gate.reshape(spt, T, tn)[:, T - 2:T, :]
    if spt == 1:
        nst_ref[0] = prev


def _ffn_up(x, w_up, layer, conv_state, conv_w, conv_b, T):
    M, D = x.shape
    F = w_up.shape[-1] // 2
    n_seq = M // T
    spt = _pick(n_seq, (8, 4, 2, 1)) if T * 8 <= 512 else 1
    tm = spt * T
    subs = [tm]
    if spt == 1 and tm >= 512 and tm % 16 == 0:
        subs, rem = [], tm
        while rem > 640:
            subs.append(512)
            rem -= 512
        subs += [rem - 128, 128] if rem > 256 else [rem]
    tn = _pick(F, (256, 128))
    nf = F // tn
    x_mode = pl.Buffered(1) if tm * D * 2 > 8 * 1024 * 1024 else None
    st_spec = pl.BlockSpec((spt, 2, tn), lambda i, j: (i, 0, j))
    return pl.pallas_call(
        functools.partial(_ffn_up_kernel, T=T, spt=spt, subs=tuple(subs)),
        grid=(M // tm, nf),
        in_specs=[pl.BlockSpec((tm, D), lambda i, j: (i, 0), pipeline_mode=x_mode),
                  pl.BlockSpec((None, D, tn), lambda i, j: (layer, 0, j)),
                  pl.BlockSpec((None, D, tn), lambda i, j: (layer, 0, j + nf)),
                  st_spec,
                  pl.BlockSpec((CONV_W, tn), lambda i, j: (0, j)),
                  pl.BlockSpec((1, tn), lambda i, j: (0, j))],
        out_specs=[pl.BlockSpec((tm, tn), lambda i, j: (i, j)), st_spec],
        out_shape=[jax.ShapeDtypeStruct((M, F), BF16), jax.ShapeDtypeStruct((n_seq, 2, F), F32)],
        compiler_params=_cparams(("parallel", "arbitrary")),
        name="ffn_up_conv",
    )(x, w_up, w_up, conv_state, conv_w, conv_b.reshape(1, F))


def _run_group(h, shift0, wkv0, conv0, past_k, past_v, p, depth, n_a):
    B, T, D = h.shape
    M = B * T
    flat = lambda t: t.reshape(M, D)
    seq = lambda t: t.reshape(B, T, D)
    gains = p['norm_gains']
    new_wkv, new_shift, new_conv = [], [], []
    v_first = None
    k_new = v_new = k_bf = v_bf = None
    h = flat(h)
    xin = None
    xkv = None
    for layer in range(depth):
        g = gains[layer]
        if layer < n_a:
            i = layer
            mixed_in, last_row = _normmix(seq(h), g[0], shift0[i], p['a_mix'][i])
            xr, xw, xk, xv, xa, xg = (flat(t) for t in mixed_in)
            new_shift.append(last_row)
            r = _matmul(xr, p['a_w_rkv'], widx=(i, 0), out_dtype=F32, name="rwkv_r")
            k = _matmul(xk, p['a_w_rkv'], widx=(i, 1), out_dtype=F32, name="rwkv_k")
            v = _matmul(xv, p['a_w_rkv'], widx=(i, 2), out_dtype=F32, name="rwkv_v")
            ld = _matmul(_matmul(xw, p['a_w1'], widx=(i,), out_dtype=BF16, epilogue=_epi_tanh, name="lora_w1"),
                         p['a_w2'], widx=(i,), out_dtype=F32, epilogue=_epi_log_decay,
                         rows=(p['a_w0'][i].reshape(1, D),), name="lora_w2")
            a = _matmul(_matmul(xa, p['a_a1'], widx=(i,), out_dtype=BF16, name="lora_a1"),
                        p['a_a2'], widx=(i,), out_dtype=F32, epilogue=_epi_bias_sigmoid,
                        rows=(p['a_a0'][i].reshape(1, D),), name="lora_a2")
            gate = _matmul(_matmul(xg, p['a_g1'], widx=(i,), out_dtype=BF16, epilogue=_epi_sigmoid,
                                   name="lora_g1"),
                           p['a_g2'], widx=(i,), out_dtype=F32, name="lora_g2")
            if i == 0:
                v_first = v
            else:
                v = _matmul(_matmul(xv, p['a_v1'], widx=(i - 1,), out_dtype=BF16, name="lora_v1"),
                            p['a_v2'], widx=(i - 1,), out_dtype=F32, epilogue=_epi_value_mix,
                            rows=(p['a_v0'][i - 1].reshape(1, D),), tiles=(v, v_first), name="lora_v2")
            y, s_out = _wkv(seq(r), seq(k), seq(v), seq(ld), seq(a), seq(gate),
                            p['a_k_k'][i], p['a_k_a'][i], p['a_r_k'][i].reshape(D), p['a_ln_w'][i],
                            p['a_ln_b'][i], wkv0[i])
            new_wkv.append(s_out)
            mixed = _matmul(flat(y), p['a_w_out'], widx=(i,), out_dtype=F32, name="rwkv_out")
        else:
            j = layer - n_a
            if layer == n_a:
                k_new, k_bf = _matmul(xkv, p['w_kv'], out_dtype=(F32, BF16), n=D, name="kv_proj_k")
                v_new, v_bf = _matmul(xkv, p['w_kv'], out_dtype=(F32, BF16), n=D, col_off=D, name="kv_proj_v")
            q = _matmul(xin, p['b_w_q'], widx=(j,), out_dtype=BF16, name="attn_q")
            attn = _attention(seq(q), seq(k_bf), seq(v_bf), past_k, past_v)
            mixed = _matmul(flat(attn), p['b_w_out'], widx=(j,), out_dtype=F32, name="attn_out")
        h, (xf,) = _resnorm(h, mixed, g[1], [g[2]])
        act, conv_new = _ffn_up(xf, p['ffn_w_up'], layer, conv0[layer], p['ffn_conv_w'][layer],
                                p['ffn_conv_b'][layer], T)
        new_conv.append(conv_new)
        f = _matmul(act, p['ffn_w_down'], widx=(layer,), out_dtype=F32, name="ffn_down")
        nxt = []
        if layer + 1 < depth and layer + 1 >= n_a:
            nxt.append(gains[layer + 1][0])
            if layer + 1 == n_a:
                nxt.append(p['kv_norm'])
        h, normed = _resnorm(h, f, g[3], nxt)
        if normed:
            xin = normed[0]
            if len(normed) > 1:
                xkv = normed[1]
    return (seq(h), seq(k_new), seq(v_new), jnp.stack(new_wkv), jnp.stack(new_shift),
            jnp.stack(new_conv))


def kernel(x_prompt, x_sample, cache_k, cache_v, state_wkv, state_shift, state_ffn_conv, meta_tokens, norm_gains, a_mix, a_w_rkv, a_w_out, a_w0, a_w1, a_w2, a_a0, a_a1, a_a2, a_v0, a_v1, a_v2, a_g1, a_g2, a_k_k, a_k_a, a_r_k, a_ln_w, a_ln_b, kv_norm, w_kv, b_w_q, b_w_out, ffn_w_up, ffn_conv_w, ffn_conv_b, ffn_w_down):
    B, S, D = x_prompt.shape
    Bs, Ts, _ = x_sample.shape
    depth = norm_gains.shape[0]
    n_a = a_mix.shape[0]
    F = ffn_w_down.shape[1]
    H_a, H_b = D // HEAD_A, D // HEAD_B
    assert n_a >= 1 and depth > n_a
    bf = lambda w: w.astype(BF16)
    p = {
        'norm_gains': norm_gains, 'a_mix': a_mix, 'a_w_rkv': a_w_rkv, 'a_w_out': a_w_out,
        'a_w0': a_w0, 'a_w1': a_w1, 'a_w2': a_w2, 'a_a0': a_a0, 'a_a1': a_a1,
        'a_a2': a_a2, 'a_v0': a_v0, 'a_v1': a_v1, 'a_v2': a_v2, 'a_g1': a_g1,
        'a_g2': a_g2, 'a_k_k': a_k_k, 'a_k_a': a_k_a, 'a_r_k': a_r_k, 'a_ln_w': a_ln_w,
        'a_ln_b': a_ln_b, 'kv_norm': kv_norm, 'w_kv': w_kv, 'b_w_q': b_w_q,
        'b_w_out': b_w_out, 'ffn_w_up': ffn_w_up, 'ffn_conv_w': ffn_conv_w,
        'ffn_conv_b': ffn_conv_b, 'ffn_w_down': bf(ffn_w_down),
    }
    dt = x_prompt.dtype
    T = N_META + S
    meta = jnp.broadcast_to(meta_tokens[None].astype(dt), (B, N_META, D))
    h0 = jnp.concatenate([meta, x_prompt], axis=1)
    h_p, k_p, v_p, wkv_p, shift_p, conv_p = _run_group(
        h0, jnp.zeros((n_a, B, D), dt), jnp.zeros((n_a, B, H_a, HEAD_A, HEAD_A), dt),
        jnp.zeros((depth, B, CONV_W - 1, F), dt), None, None, p, depth, n_a)
    P = cache_k.shape[1]
    h_s, k_s, v_s, wkv_s, shift_s, conv_s = _run_group(
        x_sample, state_shift, state_wkv, state_ffn_conv, cache_k.reshape(Bs, P, D),
        cache_v.reshape(Bs, P, D), p, depth, n_a)
    heads = lambda t: t.reshape(t.shape[0], t.shape[1], H_b, HEAD_B)
    return (h_p[:, N_META:], h_s, heads(k_p), heads(v_p), heads(k_s), heads(v_s),
            wkv_p, wkv_s, shift_p, shift_s, conv_p, conv_s)
```

```python
import functools

import jax
import jax.numpy as jnp
from jax import lax
from jax.experimental import pallas as pl
from jax.experimental.pallas import tpu as pltpu

F32 = jnp.float32
BF16 = jnp.bfloat16

N_META = 16
HEAD_A = 64
HEAD_B = 128
GN_EPS = 64e-5
RMS_EPS = 1e-6
CONV_W = 3

V7X_LANES = 128
V7X_MXU_DIM = 256
V7X_VMEM_BYTES = 64 * 1024 * 1024
VMEM_LIMIT = V7X_VMEM_BYTES - 8 * 1024 * 1024

W_CAST_CHUNK = 1024
KGRID_W_BYTES = 8 * 1024 * 1024
LOG2E = 1.4426950408889634
EXP_NEG_HALF = -0.6065306597126334

WKV_GROUP = 2
WKV_LANES = WKV_GROUP * HEAD_A
WKV_CHUNK = 64
ATT_TQ = 128
ATT_HEADS = 8


def _pick(n, prefs):
    for p in prefs:
        if p <= n and n % p == 0:
            return p
    return n


def _cparams(sem):
    return pltpu.CompilerParams(dimension_semantics=sem, vmem_limit_bytes=VMEM_LIMIT)


def _bdot(a, b):
    return jnp.dot(a.astype(BF16), b.astype(BF16), preferred_element_type=F32)


def _bdot_nt(a, b):
    return lax.dot_general(a.astype(BF16), b.astype(BF16), (((1,), (1,)), ((), ())),
                           preferred_element_type=F32)


def _bdot_tn(a, b):
    return lax.dot_general(a.astype(BF16), b.astype(BF16), (((0,), (0,)), ((), ())),
                           preferred_element_type=F32)


def _split(x, parts):
    out = []
    rem = x
    for i in range(parts):
        hi = rem.astype(BF16)
        out.append(hi)
        if i + 1 < parts:
            rem = rem - hi.astype(F32)
    return out


def _split_dot(x, m_bf16, parts):
    acc = None
    for hi in _split(x, parts):
        t = jnp.dot(hi, m_bf16, preferred_element_type=F32)
        acc = t if acc is None else acc + t
    return acc


def _split_rows_dot(xs, m_bf16, parts):
    n = xs[0].shape[0]
    stacked = jnp.concatenate([p for x in xs for p in _split(x, parts)], axis=0)
    res = jnp.dot(stacked, m_bf16, preferred_element_type=F32)
    out = []
    for i in range(len(xs)):
        acc = res[i * parts * n:(i * parts + 1) * n]
        for j in range(1, parts):
            acc = acc + res[(i * parts + j) * n:(i * parts + j + 1) * n]
        out.append(acc)
    return out


def _split_dot_lhs(m_bf16, x, parts):
    acc = None
    for hi in _split(x, parts):
        t = jnp.dot(m_bf16, hi, preferred_element_type=F32)
        acc = t if acc is None else acc + t
    return acc


def _interleave(gens):
    results = [None] * len(gens)
    live = list(range(len(gens)))
    while live:
        still = []
        for i in live:
            try:
                next(gens[i])
                still.append(i)
            except StopIteration as stop:
                results[i] = stop.value
        live = still
    return results


def _rms(x, g):
    return x * lax.rsqrt(jnp.mean(x * x, axis=-1, keepdims=True) + RMS_EPS) * g


def _epi_none(acc):
    return acc


def _epi_tanh(acc):
    return jnp.tanh(acc)


def _epi_sigmoid(acc):
    return jax.nn.sigmoid(acc)


def _epi_bias_sigmoid(acc, bias):
    return jax.nn.sigmoid(bias + acc)


def _epi_log_decay(acc, w0):
    return EXP_NEG_HALF * jax.nn.sigmoid(w0 + acc)


def _epi_value_mix(acc, v0, v, v_first):
    return v + (v_first - v) * jax.nn.sigmoid(v0 + acc)


def _w_chunks(w_ref):
    K = w_ref.shape[0]
    if w_ref.dtype == BF16 or K % W_CAST_CHUNK or K == W_CAST_CHUNK:
        return [(slice(None), w_ref[...].astype(BF16))]
    return [(slice(k0, k0 + W_CAST_CHUNK), w_ref[k0:k0 + W_CAST_CHUNK, :].astype(BF16))
            for k0 in range(0, K, W_CAST_CHUNK)]


def _dot_chunks(x_ref, rows, chunks):
    acc = None
    for ks, w in chunks:
        t = jnp.dot(x_ref[rows, ks], w, preferred_element_type=F32)
        acc = t if acc is None else acc + t
    return acc


def _dot_cast_w(x_ref, w_ref):
    return _dot_chunks(x_ref, slice(None), _w_chunks(w_ref))


def _mm_kernel(*refs, epilogue, n_row, n_tile):
    x_ref, w_ref = refs[0], refs[1]
    rows = [r[...] for r in refs[2:2 + n_row]]
    tiles = [r[...] for r in refs[2 + n_row:2 + n_row + n_tile]]
    acc = _dot_cast_w(x_ref, w_ref)
    res = epilogue(acc, *rows, *tiles)
    for o_ref in refs[2 + n_row + n_tile:]:
        o_ref[...] = res.astype(o_ref.dtype)


def _matmul_tiles(M, K, N, w_bytes, out_bytes, n_tiles):
    budget = VMEM_LIMIT - 6 * 1024 * 1024
    row_tiles = [t for t in (2064, 1376, 688, 512, 256, 128, 64, 16) if M % t == 0] or [M]
    tn_opts = [t for t in (512, 256, 128) if N % t == 0] or [N]

    def fits(tm, tn, x_bufs):
        need = (x_bufs * tm * K * 2 + 2 * K * tn * w_bytes + (W_CAST_CHUNK * tn * 2 if w_bytes > 2 else 0)
                + 2 * tm * tn * (out_bytes + 4 * n_tiles) + tm * tn * 4)
        return need <= budget

    for tn in tn_opts:
        for tm in row_tiles:
            if tm >= min(1024, row_tiles[0]) and fits(tm, tn, 2):
                return tm, tn, 2
        for tm in row_tiles:
            if fits(tm, tn, 1):
                return tm, tn, 1
    return row_tiles[-1], tn_opts[-1], 1


def _mmk_kernel(*refs, epilogue, n_row, n_tile, n_out):
    x_ref, w_ref = refs[0], refs[1]
    row_refs = refs[2:2 + n_row]
    tile_refs = refs[2 + n_row:2 + n_row + n_tile]
    out_refs = refs[2 + n_row + n_tile:2 + n_row + n_tile + n_out]
    acc_ref = refs[-1]
    k = pl.program_id(0)
    part = jnp.dot(x_ref[...], w_ref[...].astype(BF16), preferred_element_type=F32)

    @pl.when(k == 0)
    def _():
        acc_ref[...] = part

    @pl.when(k > 0)
    def _():
        acc_ref[...] += part

    @pl.when(k == pl.num_programs(0) - 1)
    def _():
        res = epilogue(acc_ref[...], *[r[...] for r in row_refs], *[t[...] for t in tile_refs])
        for o_ref in out_refs:
            o_ref[...] = res.astype(o_ref.dtype)


def _matmul_kgrid(x, w, dtypes, epilogue, rows, tiles, widx, N, col_off, name, as_tuple):
    M, K = x.shape
    tk = _pick(K, tuple(t for t in (2048, 1024, 512, 256, 128) if t * N * w.dtype.itemsize <= KGRID_W_BYTES))
    jblk = col_off // N
    full = pl.BlockSpec((M, N), lambda k: (0, 0))
    in_specs = [pl.BlockSpec((M, tk), lambda k: (0, k)),
                pl.BlockSpec((None,) * len(widx) + (tk, N), lambda k: (*widx, k, jblk))]
    in_specs += [pl.BlockSpec((1, N), lambda k: (0, 0)) for _ in rows] + [full for _ in tiles]
    outs = pl.pallas_call(
        functools.partial(_mmk_kernel, epilogue=epilogue, n_row=len(rows), n_tile=len(tiles), n_out=len(dtypes)),
        grid=(K // tk,),
        in_specs=in_specs,
        out_specs=[full for _ in dtypes],
        out_shape=[jax.ShapeDtypeStruct((M, N), d) for d in dtypes],
        scratch_shapes=[pltpu.VMEM((M, N), F32)],
        compiler_params=_cparams(("arbitrary",)),
        name=name,
    )(x, w, *rows, *tiles)
    return outs if as_tuple else outs[0]


def _matmul(x, w, *, out_dtype, epilogue=_epi_none, rows=(), tiles=(), widx=(), n=None, col_off=0, name):
    M, K = x.shape
    N = w.shape[-1] if n is None else n
    dtypes = out_dtype if isinstance(out_dtype, tuple) else (out_dtype,)
    if M <= 512 and w.dtype == F32 and K * N * 4 >= 2 * KGRID_W_BYTES and col_off % N == 0:
        return _matmul_kgrid(x, w, dtypes, epilogue, rows, tiles, widx, N, col_off, name,
                             isinstance(out_dtype, tuple))
    tm, tn, x_bufs = _matmul_tiles(M, K, N, w.dtype.itemsize, sum(jnp.dtype(d).itemsize for d in dtypes),
                                   len(tiles))
    assert col_off % tn == 0 and w.ndim == 2 + len(widx)
    joff = col_off // tn
    x_mode = pl.Buffered(1) if x_bufs == 1 else None
    in_specs = [pl.BlockSpec((tm, K), lambda i, j: (i, 0), pipeline_mode=x_mode),
                pl.BlockSpec((None,) * len(widx) + (K, tn), lambda i, j: (*widx, 0, j + joff))]
    in_specs += [pl.BlockSpec((1, tn), lambda i, j: (0, j)) for _ in rows]
    in_specs += [pl.BlockSpec((tm, tn), lambda i, j: (i, j)) for _ in tiles]
    outs = pl.pallas_call(
        functools.partial(_mm_kernel, epilogue=epilogue, n_row=len(rows), n_tile=len(tiles)),
        grid=(M // tm, N // tn),
        in_specs=in_specs,
        out_specs=[pl.BlockSpec((tm, tn), lambda i, j: (i, j)) for _ in dtypes],
        out_shape=[jax.ShapeDtypeStruct((M, N), d) for d in dtypes],
        compiler_params=_cparams(("parallel", "arbitrary")),
        name=name,
    )(x, w, *rows, *tiles)
    return outs if isinstance(out_dtype, tuple) else outs[0]


def _resnorm_kernel(*refs, n_next):
    h_ref, y_ref, gp_ref = refs[0], refs[1], refs[2]
    gn_refs = refs[3:3 + n_next]
    hn_ref = refs[3 + n_next]
    xo_refs = refs[4 + n_next:]
    hn = h_ref[...] + _rms(y_ref[...], gp_ref[...])
    hn_ref[...] = hn
    for gn_ref, xo_ref in zip(gn_refs, xo_refs):
        xo_ref[...] = _rms(hn, gn_ref[...]).astype(xo_ref.dtype)


def _resnorm(h, y, g_post, g_next):
    M, D = h.shape
    tr = _pick(M, (192, 128, 64, 16))
    n = len(g_next)
    row = pl.BlockSpec((tr, D), lambda i: (i, 0))
    vec = pl.BlockSpec((1, D), lambda i: (0, 0))
    outs = pl.pallas_call(
        functools.partial(_resnorm_kernel, n_next=n),
        grid=(M // tr,),
        in_specs=[row, row, vec] + [vec] * n,
        out_specs=[row] + [row] * n,
        out_shape=[jax.ShapeDtypeStruct((M, D), F32)] + [jax.ShapeDtypeStruct((M, D), BF16)] * n,
        compiler_params=_cparams(("parallel",)),
        name="resnorm",
    )(h, y, g_post.reshape(1, D), *[g.reshape(1, D) for g in g_next])
    return outs[0], list(outs[1:])


def _normmix_kernel(h_ref, halo_ref, g_ref, shift_ref, mix_ref, *out_refs, tt):
    t = pl.program_id(1)
    g = g_ref[...]
    xn = _rms(h_ref[0], g)
    halo = _rms(halo_ref[0], g)
    prev_row = jnp.where(t == 0, shift_ref[0], halo[7:8, :])
    row = lax.broadcasted_iota(jnp.int32, (tt, 1), 0)
    x_prev = jnp.where(row == 0, prev_row, pltpu.roll(xn, 1, axis=0))
    xx = x_prev - xn
    for i in range(6):
        out_refs[i][0] = (xn + xx * mix_ref[i:i + 1, :]).astype(BF16)
    out_refs[6][0] = xn[tt - 1:tt, :]


def _normmix(h, g, shift0, mix):
    B, T, D = h.shape
    tt = _pick(T, (64, 48, 32, 16))
    nb = tt // 8
    tok = pl.BlockSpec((1, tt, D), lambda b, t: (b, t, 0))
    outs = pl.pallas_call(
        functools.partial(_normmix_kernel, tt=tt),
        grid=(B, T // tt),
        in_specs=[tok,
                  pl.BlockSpec((1, 8, D), lambda b, t: (b, jnp.maximum(t * nb - 1, 0), 0)),
                  pl.BlockSpec((1, D), lambda b, t: (0, 0)),
                  pl.BlockSpec((1, 1, D), lambda b, t: (b, 0, 0)),
                  pl.BlockSpec((6, D), lambda b, t: (0, 0))],
        out_specs=[tok] * 6 + [pl.BlockSpec((1, 1, D), lambda b, t: (b, 0, 0))],
        out_shape=[jax.ShapeDtypeStruct((B, T, D), BF16)] * 6 + [jax.ShapeDtypeStruct((B, 1, D), F32)],
        compiler_params=_cparams(("parallel", "arbitrary")),
        name="normmix",
    )(h, h, g.reshape(1, D), shift0.reshape(B, 1, D), mix)
    return list(outs[:6]), outs[6].reshape(B, D)


def _wkv_phase1(c0, n_valid, lanes, refs, prm, bones):
    r_ref, k_ref, v_ref, ld_ref, a_ref = refs
    kkw, kaw, rkw = prm
    G, LW, C = WKV_GROUP, WKV_LANES, WKV_CHUNK
    R = G * C
    sl = pl.ds(c0, C)
    r = r_ref[0, sl, lanes]
    k = k_ref[0, sl, lanes]
    v = v_ref[0, sl, lanes]
    ld = ld_ref[0, sl, lanes]
    a = a_ref[0, sl, lanes]
    if n_valid < C:
        valid = lax.broadcasted_iota(jnp.int32, (C, 1), 0) < n_valid
        r, k, v, ld = (jnp.where(valid, x, 0.0) for x in (r, k, v, ld))

    kk = k * kkw
    km = k * (1.0 + (a - 1.0) * kaw)
    ti = lax.broadcasted_iota(jnp.int32, (C, C), 0)
    si = lax.broadcasted_iota(jnp.int32, (C, C), 1)
    sums = _split_rows_dot([kk * kk, r * km * rkw], bones, 2)
    kk_ss, rk_sum = sums[0], sums[1]
    tri = jnp.where(ti >= si, 1.0, 0.0).astype(BF16)
    cum3 = jnp.dot(tri, jnp.concatenate(_split(ld, 3), axis=1), preferred_element_type=F32)
    cum = cum3[:, :LW] + cum3[:, LW:2 * LW] + cum3[:, 2 * LW:]
    yield
    kk = kk / jnp.maximum(jnp.sqrt(kk_ss), 1e-12)
    bonus = rk_sum * v
    b = kk * a
    cum_last = cum[C - 1:C, :]
    e_neg = jnp.exp(-cum)
    e_rem = jnp.exp(cum_last - cum)
    lane_head = lax.broadcasted_iota(jnp.int32, (1, LW), 1) // HEAD_A

    def stack(x):
        return jnp.concatenate([jnp.where(lane_head == h, x, 0.0) for h in range(G)], axis=0)

    r_s = stack(jnp.exp(cum) * r)
    a_b = stack(jnp.exp(cum - ld) * (-kk)).astype(BF16)
    v_b = stack(v).astype(BF16)
    ar_b = jnp.concatenate([a_b, r_s.astype(BF16)], axis=0)
    bk_b = jnp.concatenate([stack(e_neg * b), stack(e_neg * km)], axis=0).astype(BF16)
    bkh_t = jnp.concatenate([stack(e_rem * b).T, stack(e_rem * km).T], axis=1).astype(BF16)

    big = _bdot_nt(ar_b, bk_b)
    yield
    row = lax.broadcasted_iota(jnp.int32, (R, R), 0)
    col = lax.broadcasted_iota(jnp.int32, (R, R), 1)
    same = (row // C) == (col // C)
    strict = same & (row > col)
    incl = same & (row >= col)
    n_ab = jnp.where(strict, big[:R, :R], 0.0)
    a_ak = jnp.where(strict, big[:R, R:], 0.0)
    a_rb = jnp.where(incl, big[R:, :R], 0.0).astype(BF16)
    a_rk = jnp.where(incl, big[R:, R:], 0.0)
    av = _bdot(jnp.concatenate([a_ak, a_rk], axis=0), v_b)

    tinv = jnp.where(row == col, 1.0, 0.0) + n_ab
    npow = n_ab
    span = 2
    while span < C:
        npow_b = npow.astype(BF16)
        if span > 2:
            both = _bdot(jnp.concatenate([npow_b, tinv.astype(BF16)], axis=0), npow_b)
            npow, tinv = both[:R], tinv + both[R:]
        else:
            npow = _bdot(npow_b, npow_b)
        span *= 2
        yield
    tinv = tinv + _bdot(tinv, npow)
    yield

    wu_b = _bdot(tinv, jnp.concatenate([a_b, av[:R].astype(BF16)], axis=1)).astype(BF16)
    yield
    qo = _bdot(a_rb, wu_b)
    low = jnp.concatenate([jnp.zeros((R, LW), BF16), v_b], axis=1)
    m = _bdot(bkh_t, jnp.concatenate([wu_b, low], axis=0))
    yield
    q_b = (r_s + qo[:, :LW]).astype(BF16)
    o0 = qo[:, LW:] + av[R:]
    li = lax.broadcasted_iota(jnp.int32, (LW, LW), 0)
    lj = lax.broadcasted_iota(jnp.int32, (LW, LW), 1)
    m1_b = (jnp.where(li == lj, jnp.exp(cum_last), 0.0) + m[:, :LW]).astype(BF16)
    return jnp.concatenate([q_b, m1_b], axis=0), o0, m[:, LW:], bonus


def _wkv_phase2(state, ph, c0, n_valid, lanes, g_ref, y_ref, lnw, lnb, bones):
    qm_b, o0, m2, bonus = ph
    C = WKV_CHUNK
    R = WKV_GROUP * C
    qs = _bdot(qm_b, state)
    yield
    o_st = qs[:R] + o0
    new_state = qs[R:] + m2
    o = o_st[0:C, :]
    for h in range(1, WKV_GROUP):
        o = o + o_st[h * C:(h + 1) * C, :]
    mu = _split_rows_dot([o], bones, 2)[0] * (1.0 / HEAD_A)
    yield
    d = o - mu
    var = _split_rows_dot([d * d], bones, 2)[0] * (1.0 / HEAD_A)
    yield
    o = d * lax.rsqrt(var + GN_EPS) * lnw + lnb + bonus
    sl = pl.ds(c0, n_valid)
    y_ref[0, sl, lanes] = (o[:n_valid] * g_ref[0, sl, lanes]).astype(y_ref.dtype)
    return new_state


def _wkv_kernel(r_ref, k_ref, v_ref, ld_ref, a_ref, g_ref, kkw_ref, kaw_ref, rkw_ref, lnw_ref, lnb_ref,
                s0_ref, y_ref, sout_ref, *, T, n_tiles, n_par):
    G, LW, N = WKV_GROUP, WKV_LANES, HEAD_A
    li = lax.broadcasted_iota(jnp.int32, (LW, LW), 0)
    lj = lax.broadcasted_iota(jnp.int32, (LW, LW), 1)
    bones = jnp.where((li // N) == (lj // N), 1.0, 0.0).astype(BF16)
    refs = (r_ref, k_ref, v_ref, ld_ref, a_ref)
    lane_sl = [slice(p * LW, (p + 1) * LW) for p in range(n_tiles)]
    prm = [(kkw_ref[:, s], kaw_ref[:, s], rkw_ref[:, s]) for s in lane_sl]
    post = [(lnw_ref[:, s], lnb_ref[:, s]) for s in lane_sl]

    def steps(c_base, n_valid, n_chunks, states):
        C = WKV_CHUNK
        ph = _interleave([_wkv_phase1(c_base + u * C, n_valid, lane_sl[p], refs, prm[p], bones)
                          for u in range(n_chunks) for p in range(n_tiles)])
        states = list(states)
        for u in range(n_chunks):
            states = _interleave([
                _wkv_phase2(states[p].astype(BF16), ph[u * n_tiles + p], c_base + u * C, n_valid, lane_sl[p],
                            g_ref, y_ref, post[p][0], post[p][1], bones) for p in range(n_tiles)])
        return tuple(states)

    zero = jnp.zeros((N, N), F32)
    states = []
    for p in range(n_tiles):
        blocks = [jnp.concatenate([s0_ref[0, p * G + h].T if j == h else zero for j in range(G)], axis=1)
                  for h in range(G)]
        states.append(jnp.concatenate(blocks, axis=0))
    states = tuple(states)

    head = T % WKV_CHUNK
    if head:
        states = steps(0, head, 1, states)
    n_main = T // WKV_CHUNK
    if n_main:
        def body(i, st):
            return steps(pl.multiple_of(head + i * (n_par * WKV_CHUNK), 16), WKV_CHUNK, n_par, st)

        states = lax.fori_loop(0, n_main // n_par, body, states)

    for p in range(n_tiles):
        for h in range(G):
            sout_ref[0, p * G + h] = states[p][h * N:(h + 1) * N, h * N:(h + 1) * N].T


def _wkv(r, k, v, ld, a, g, kkw, kaw, rkw, lnw, lnb, s0):
    B, T, D = r.shape
    LW, G = WKV_LANES, WKV_GROUP
    n_main = T // WKV_CHUNK
    assert T >= WKV_CHUNK
    n_par = _pick(n_main, (4, 2, 1))
    n_tiles = _pick(D // LW, (2,)) if n_main > 1 else _pick(D // LW, (8, 4, 2))
    bw = n_tiles * LW
    tok = pl.BlockSpec((1, T, bw), lambda b, j: (b, 0, j))
    vec = pl.BlockSpec((1, bw), lambda b, j: (0, j))
    st = pl.BlockSpec((1, n_tiles * G, HEAD_A, HEAD_A), lambda b, j: (b, j, 0, 0))
    y, s_out = pl.pallas_call(
        functools.partial(_wkv_kernel, T=T, n_tiles=n_tiles, n_par=n_par),
        grid=(B, D // bw),
        in_specs=[tok] * 6 + [vec] * 5 + [st],
        out_specs=[tok, st],
        out_shape=[jax.ShapeDtypeStruct((B, T, D), BF16), jax.ShapeDtypeStruct(s0.shape, F32)],
        compiler_params=_cparams(("parallel", "parallel")),
        name="wkv7",
    )(r, k, v, ld, a, g, *[p.reshape(1, D) for p in (kkw, kaw, rkw, lnw, lnb)], s0)
    return y, s_out


def _sb_consts(nk):
    ji = lax.broadcasted_iota(jnp.int32, (nk, nk), 0)
    si = lax.broadcasted_iota(jnp.int32, (nk, nk), 1)
    later = jnp.where(ji > si, -1.0, 0.0).astype(BF16)
    return jnp.concatenate([later, later], axis=0)


def _sb_pair(q, kt, vt, carry, acc, later2, read):
    nq, nk = q.shape[0], kt.shape[0]
    z = _bdot_nt(q, kt) * (HEAD_B ** -0.5)
    yield
    sp = jnp.maximum(z, 0.0) + jnp.log(1.0 + jnp.exp2(jnp.abs(z) * (-LOG2E)))
    spm = sp if read is None else jnp.where(read, sp, 0.0)
    hi, lo = _split(spm, 2)
    if nk % V7X_LANES == 0:
        local = jnp.dot(jnp.concatenate([hi, lo], axis=1), later2, preferred_element_type=F32)
    else:
        local = (jnp.dot(hi, later2[:nk], preferred_element_type=F32)
                 + jnp.dot(lo, later2[:nk], preferred_element_type=F32))
    yield
    total = jnp.broadcast_to(local[:, 0:1] - spm[:, 0:1], (nq, V7X_LANES))
    if nk % V7X_LANES == 0:
        after = local + jnp.concatenate([carry] * (nk // V7X_LANES), axis=1)
    else:
        after = local + carry[:, :nk]
    w = jnp.exp((z - sp) + after)
    if read is not None:
        w = jnp.where(read, w, 0.0)
    return carry + total, acc + _bdot(w, vt)


def _attn_kernel(*refs, T, P, KN, KP):
    if P:
        q_ref, kn_ref, vn_ref, kp_ref, vp_ref, o_ref = refs
    else:
        q_ref, kn_ref, vn_ref, o_ref = refs
    TQ, HB = ATT_TQ, HEAD_B
    n_heads = q_ref.shape[2] // HB
    lanes = [slice(h * HB, (h + 1) * HB) for h in range(n_heads)]
    n_full, rem = T // TQ, T % TQ
    c_new = _sb_consts(KN)
    c_past = _sb_consts(KP) if P else None

    def sweep(qs, ca, n_new):
        def step(k_ref, v_ref, width, consts, jj, ca):
            rows = pl.ds(pl.multiple_of(jj * width, width), width)
            return tuple(_interleave([_sb_pair(qs[h], k_ref[0, rows, lanes[h]].astype(BF16),
                                               v_ref[0, rows, lanes[h]].astype(BF16),
                                               ca[h][0], ca[h][1], consts, None) for h in range(n_heads)]))

        ca = lax.fori_loop(0, n_new, lambda j, c: step(kn_ref, vn_ref, KN, c_new, n_new - 1 - j, c), ca)
        if P:
            n_past = P // KP
            ca = lax.fori_loop(0, n_past, lambda j, c: step(kp_ref, vp_ref, KP, c_past, n_past - 1 - j, c), ca)
        return ca

    def query_tile(q0, nq, k0, nk, off, consts):
        qrows = pl.ds(q0, nq)
        krows = pl.ds(k0, nk)
        qs = [q_ref[0, qrows, lanes[h]] for h in range(n_heads)]
        cmr = lax.broadcasted_iota(jnp.int32, (nq, nk), 1) - lax.broadcasted_iota(jnp.int32, (nq, nk), 0)
        read = cmr < off
        zc = jnp.zeros((nq, V7X_LANES), F32)
        za = jnp.zeros((nq, HB), F32)
        ca = tuple(_interleave([_sb_pair(qs[h], kn_ref[0, krows, lanes[h]], vn_ref[0, krows, lanes[h]],
                                         zc, za, consts, read) for h in range(n_heads)]))
        ca = sweep(qs, ca, k0 // KN)
        for h in range(n_heads):
            o_ref[0, qrows, lanes[h]] = ca[h][1].astype(o_ref.dtype)

    if n_full and KN == 2 * TQ and n_full % 2 == 0:
        c_half = _sb_consts(TQ)

        def tile_pair(i, c):
            q0 = pl.multiple_of(i * KN, KN)
            query_tile(q0, TQ, q0, TQ, 0, c_half)
            query_tile(q0 + TQ, TQ, q0, KN, TQ, c_new)
            return c

        lax.fori_loop(0, n_full // 2, tile_pair, 0)
    elif n_full:
        def full_tile(i, c):
            q0 = pl.multiple_of(i * TQ, TQ)
            k0 = pl.multiple_of((q0 // KN) * KN, KN)
            query_tile(q0, TQ, k0, KN, q0 - k0, c_new)
            return c

        lax.fori_loop(0, n_full, full_tile, 0)
    if rem:
        q0 = n_full * TQ
        query_tile(q0, rem, q0, rem, 0, _sb_consts(rem))


def _attention(q, k_new, v_new, k_past, v_past):
    B, T, D = q.shape
    P = 0 if k_past is None else k_past.shape[1]
    full = T - T % ATT_TQ
    KN = V7X_MXU_DIM if full % V7X_MXU_DIM == 0 and full else ATT_TQ
    KP = V7X_MXU_DIM if P % V7X_MXU_DIM == 0 else ATT_TQ
    assert P % KP == 0 and full % KN == 0
    bw = HEAD_B * _pick(D // HEAD_B, (ATT_HEADS, 2, 1))
    new = pl.BlockSpec((1, T, bw), lambda b, h: (b, 0, h))
    in_specs = [new, new, new]
    args = [q, k_new, v_new]
    if P:
        past = pl.BlockSpec((1, P, bw), lambda b, h: (b, 0, h))
        in_specs += [past, past]
        args += [k_past, v_past]
    return pl.pallas_call(
        functools.partial(_attn_kernel, T=T, P=P, KN=KN, KP=KP),
        grid=(B, D // bw),
        in_specs=in_specs,
        out_specs=new,
        out_shape=jax.ShapeDtypeStruct((B, T, D), BF16),
        compiler_params=_cparams(("parallel", "parallel")),
        name="sb_attention",
    )(*args)


def _ffn_up_kernel(x_ref, wg_ref, wu_ref, st_ref, cw_ref, cb_ref, act_ref, nst_ref, *, T, spt, subs):
    tm, tn = act_ref.shape
    wg = _w_chunks(wg_ref)
    wu = _w_chunks(wu_ref)
    cw = cw_ref[...]
    cb = cb_ref[...]
    st = st_ref[...]
    if spt == 1:
        s0, s1 = st[0, 0:1, :], st[0, 1:2, :]
    else:
        s0 = jnp.broadcast_to(st[:, 0:1, :], (spt, T, tn)).reshape(tm, tn)
        s1 = jnp.broadcast_to(st[:, 1:2, :], (spt, T, tn)).reshape(tm, tn)
    prev = None
    r0 = 0
    for rs in subs:
        rows = slice(r0, r0 + rs)
        lrow = lax.broadcasted_iota(jnp.int32, (rs, 1), 0)
        gate = _dot_chunks(x_ref, rows, wg)
        up = _dot_chunks(x_ref, rows, wu)
        g1 = pltpu.roll(gate, 1, axis=0)
        g2 = pltpu.roll(gate, 2, axis=0)
        if prev is not None:
            g1 = jnp.where(lrow == 0, prev[1:2, :], g1)
            g2 = jnp.where(lrow == 0, prev[0:1, :], jnp.where(lrow == 1, prev[1:2, :], g2))
        pos = (lrow + r0) % T
        g1 = jnp.where(pos == 0, s1, g1)
        g2 = jnp.where(pos == 0, s0, jnp.where(pos == 1, s1, g2))
        conv = g2 * cw[0:1, :] + g1 * cw[1:2, :] + gate * cw[2:3, :] + cb
        act_ref[rows, :] = (jax.nn.silu(conv) * up).astype(act_ref.dtype)
        prev = gate[rs - 2:rs, :]
        r0 += rs
        if spt > 1:
            nst_ref[...] = gate.reshape(spt, T, tn)[:, T - 2:T, :]
    if spt == 1:
        nst_ref[0] = prev


def _ffn_up(x, w_up, layer, conv_state, conv_w, conv_b, T):
    M, D = x.shape
    F = w_up.shape[-1] // 2
    n_seq = M // T
    spt = _pick(n_seq, (8, 4, 2, 1)) if T * 8 <= 512 else 1
    tm = spt * T
    subs = [tm]
    if spt == 1 and tm >= 512 and tm % 16 == 0:
        subs, rem = [], tm
        while rem > 640:
            subs.append(512)
            rem -= 512
        subs += [rem - 128, 128] if rem > 256 else [rem]
    tn = _pick(F, (256, 128))
    nf = F // tn
    x_mode = pl.Buffered(1) if tm * D * 2 > 8 * 1024 * 1024 else None
    st_spec = pl.BlockSpec((spt, 2, tn), lambda i, j: (i, 0, j))
    return pl.pallas_call(
        functools.partial(_ffn_up_kernel, T=T, spt=spt, subs=tuple(subs)),
        grid=(M // tm, nf),
        in_specs=[pl.BlockSpec((tm, D), lambda i, j: (i, 0), pipeline_mode=x_mode),
                  pl.BlockSpec((None, D, tn), lambda i, j: (layer, 0, j)),
                  pl.BlockSpec((None, D, tn), lambda i, j: (layer, 0, j + nf)),
                  st_spec,
                  pl.BlockSpec((CONV_W, tn), lambda i, j: (0, j)),
                  pl.BlockSpec((1, tn), lambda i, j: (0, j))],
        out_specs=[pl.BlockSpec((tm, tn), lambda i, j: (i, j)), st_spec],
        out_shape=[jax.ShapeDtypeStruct((M, F), BF16), jax.ShapeDtypeStruct((n_seq, 2, F), F32)],
        compiler_params=_cparams(("parallel", "arbitrary")),
        name="ffn_up_conv",
    )(x, w_up, w_up, conv_state, conv_w, conv_b.reshape(1, F))


def _run_group(h, shift0, wkv0, conv0, past_k, past_v, p, depth, n_a):
    B, T, D = h.shape
    M = B * T
    flat = lambda t: t.reshape(M, D)
    seq = lambda t: t.reshape(B, T, D)
    gains = p['norm_gains']
    new_wkv, new_shift, new_conv = [], [], []
    v_first = None
    k_new = v_new = k_bf = v_bf = None
    h = flat(h)
    xin = None
    xkv = None
    for layer in range(depth):
        g = gains[layer]
        if layer < n_a:
            i = layer
            mixed_in, last_row = _normmix(seq(h), g[0], shift0[i], p['a_mix'][i])
            xr, xw, xk, xv, xa, xg = (flat(t) for t in mixed_in)
            new_shift.append(last_row)
            r = _matmul(xr, p['a_w_rkv'], widx=(i, 0), out_dtype=F32, name="rwkv_r")
            k = _matmul(xk, p['a_w_rkv'], widx=(i, 1), out_dtype=F32, name="rwkv_k")
            v = _matmul(xv, p['a_w_rkv'], widx=(i, 2), out_dtype=F32, name="rwkv_v")
            ld = _matmul(_matmul(xw, p['a_w1'], widx=(i,), out_dtype=BF16, epilogue=_epi_tanh, name="lora_w1"),
                         p['a_w2'], widx=(i,), out_dtype=F32, epilogue=_epi_log_decay,
                         rows=(p['a_w0'][i].reshape(1, D),), name="lora_w2")
            a = _matmul(_matmul(xa, p['a_a1'], widx=(i,), out_dtype=BF16, name="lora_a1"),
                        p['a_a2'], widx=(i,), out_dtype=F32, epilogue=_epi_bias_sigmoid,
                        rows=(p['a_a0'][i].reshape(1, D),), name="lora_a2")
            gate = _matmul(_matmul(xg, p['a_g1'], widx=(i,), out_dtype=BF16, epilogue=_epi_sigmoid,
                                   name="lora_g1"),
                           p['a_g2'], widx=(i,), out_dtype=F32, name="lora_g2")
            if i == 0:
                v_first = v
            else:
                v = _matmul(_matmul(xv, p['a_v1'], widx=(i - 1,), out_dtype=BF16, name="lora_v1"),
                            p['a_v2'], widx=(i - 1,), out_dtype=F32, epilogue=_epi_value_mix,
                            rows=(p['a_v0'][i - 1].reshape(1, D),), tiles=(v, v_first), name="lora_v2")
            y, s_out = _wkv(seq(r), seq(k), seq(v), seq(ld), seq(a), seq(gate),
                            p['a_k_k'][i], p['a_k_a'][i], p['a_r_k'][i].reshape(D), p['a_ln_w'][i],
                            p['a_ln_b'][i], wkv0[i])
            new_wkv.append(s_out)
            mixed = _matmul(flat(y), p['a_w_out'], widx=(i,), out_dtype=F32, name="rwkv_out")
        else:
            j = layer - n_a
            if layer == n_a:
                k_new, k_bf = _matmul(xkv, p['w_kv'], out_dtype=(F32, BF16), n=D, name="kv_proj_k")
                v_new, v_bf = _matmul(xkv, p['w_kv'], out_dtype=(F32, BF16), n=D, col_off=D, name="kv_proj_v")
            q = _matmul(xin, p['b_w_q'], widx=(j,), out_dtype=BF16, name="attn_q")
            attn = _attention(seq(q), seq(k_bf), seq(v_bf), past_k, past_v)
            mixed = _matmul(flat(attn), p['b_w_out'], widx=(j,), out_dtype=F32, name="attn_out")
        h, (xf,) = _resnorm(h, mixed, g[1], [g[2]])
        act, conv_new = _ffn_up(xf, p['ffn_w_up'], layer, conv0[layer], p['ffn_conv_w'][layer],
                                p['ffn_conv_b'][layer], T)
        new_conv.append(conv_new)
        f = _matmul(act, p['ffn_w_down'], widx=(layer,), out_dtype=F32, name="ffn_down")
        nxt = []
        if layer + 1 < depth and layer + 1 >= n_a:
            nxt.append(gains[layer + 1][0])
            if layer + 1 == n_a:
                nxt.append(p['kv_norm'])
        h, normed = _resnorm(h, f, g[3], nxt)
        if normed:
            xin = normed[0]
            if len(normed) > 1:
                xkv = normed[1]
    return (seq(h), seq(k_new), seq(v_new), jnp.stack(new_wkv), jnp.stack(new_shift),
            jnp.stack(new_conv))


def kernel(x_prompt, x_sample, cache_k, cache_v, state_wkv, state_shift, state_ffn_conv, meta_tokens, norm_gains, a_mix, a_w_rkv, a_w_out, a_w0, a_w1, a_w2, a_a0, a_a1, a_a2, a_v0, a_v1, a_v2, a_g1, a_g2, a_k_k, a_k_a, a_r_k, a_ln_w, a_ln_b, kv_norm, w_kv, b_w_q, b_w_out, ffn_w_up, ffn_conv_w, ffn_conv_b, ffn_w_down):
    B, S, D = x_prompt.shape
    Bs, Ts, _ = x_sample.shape
    depth = norm_gains.shape[0]
    n_a = a_mix.shape[0]
    F = ffn_w_down.shape[1]
    H_a, H_b = D // HEAD_A, D // HEAD_B
    assert n_a >= 1 and depth > n_a
    bf = lambda w: w.astype(BF16)
    p = {
        'norm_gains': norm_gains, 'a_mix': a_mix, 'a_w_rkv': a_w_rkv, 'a_w_out': a_w_out,
        'a_w0': a_w0, 'a_w1': a_w1, 'a_w2': a_w2, 'a_a0': a_a0, 'a_a1': a_a1,
        'a_a2': a_a2, 'a_v0': a_v0, 'a_v1': a_v1, 'a_v2': a_v2, 'a_g1': a_g1,
        'a_g2': a_g2, 'a_k_k': a_k_k, 'a_k_a': a_k_a, 'a_r_k': a_r_k, 'a_ln_w': a_ln_w,
        'a_ln_b': a_ln_b, 'kv_norm': kv_norm, 'w_kv': w_kv, 'b_w_q': b_w_q,
        'b_w_out': b_w_out, 'ffn_w_up': ffn_w_up, 'ffn_conv_w': ffn_conv_w,
        'ffn_conv_b': ffn_conv_b, 'ffn_w_down': bf(ffn_w_down),
    }
    dt = x_prompt.dtype
    T = N_META + S
    meta = jnp.broadcast_to(meta_tokens[None].astype(dt), (B, N_META, D))
    h0 = jnp.concatenate([meta, x_prompt], axis=1)
    h_p, k_p, v_p, wkv_p, shift_p, conv_p = _run_group(
        h0, jnp.zeros((n_a, B, D), dt), jnp.zeros((n_a, B, H_a, HEAD_A, HEAD_A), dt),
        jnp.zeros((depth, B, CONV_W - 1, F), dt), None, None, p, depth, n_a)
    P = cache_k.shape[1]
    h_s, k_s, v_s, wkv_s, shift_s, conv_s = _run_group(
        x_sample, state_shift, state_wkv, state_ffn_conv, cache_k.reshape(Bs, P, D),
        cache_v.reshape(Bs, P, D), p, depth, n_a)
    heads = lambda t: t.reshape(t.shape[0], t.shape[1], H_b, HEAD_B)
    return (h_p[:, N_META:], h_s, heads(k_p), heads(v_p), heads(k_s), heads(v_s),
            wkv_p, wkv_s, shift_p, shift_s, conv_p, conv_s)
```
